```python
import math
import jax, jax.numpy as jnp
from jax import lax
import numpy as np

D_MODEL = 1024
BATCH = 2
SEQ = 8192
DEPTH = 4

CHUNK = 64
N_MIXERS = 3
N_SB_LAYERS = (DEPTH + 2) // 3
N_S5_LAYERS = (DEPTH + 1) // 3
N_CV_LAYERS = DEPTH // 3
SB_HEADS = 16
SB_HEAD_DIM = D_MODEL // SB_HEADS
Q_BLOCK = 128
S5_GROUP = 16
S5_GROUPS = D_MODEL // S5_GROUP
S5_STATE = 64
S5_DT_MIN = 1e-3
S5_DT_MAX = 1e-1
CONV_WIDTH = 31
D_FF = ((8 * D_MODEL + 2) // 3 + 255) // 256 * 256
EPS = 1e-6

kernel_name = "hybrid_stickbreak_s5_conformer_trunk"


def rms_norm(x, g):
    xf = x.astype(jnp.float32)
    y = xf * lax.rsqrt(jnp.mean(xf * xf, axis=-1, keepdims=True) + EPS)
    return (y * g.astype(jnp.float32)).astype(x.dtype)


def modulate(h, shift, scale):
    return h * (1 + scale[:, None, :]) + shift[:, None, :]


def stick_breaking_attention(u, w_qkv, w_o):
    bsz, seq, _ = u.shape
    q, k, v = jnp.split(u @ w_qkv, 3, axis=-1)
    to_heads = lambda t: t.reshape(bsz, seq, SB_HEADS, SB_HEAD_DIM).transpose(0, 2, 1, 3)
    q, k, v = to_heads(q), to_heads(k), to_heads(v)
    scale = SB_HEAD_DIM ** -0.5
    outs = []
    for blk in range(seq // Q_BLOCK):
        q0, q1 = blk * Q_BLOCK, (blk + 1) * Q_BLOCK
        qb, kb, vb = q[:, :, q0:q1], k[:, :, :q1], v[:, :, :q1]
        z = jnp.einsum('bhqd,bhkd->bhqk', qb, kb).astype(jnp.float32) * scale
        t_idx = q0 + jnp.arange(Q_BLOCK)[:, None]
        s_idx = jnp.arange(q1)[None, :]
        mask = s_idx < t_idx
        log_beta = jax.nn.log_sigmoid(z)
        log_keep = jnp.where(mask, jax.nn.log_sigmoid(-z), 0.0)
        log_later = lax.cumsum(log_keep, axis=3, reverse=True) - log_keep
        w = jnp.where(mask, jnp.exp(log_beta + log_later), 0.0)
        outs.append(jnp.einsum('bhqk,bhkd->bhqd', w.astype(vb.dtype), vb))
    o = jnp.concatenate(outs, axis=2).transpose(0, 2, 1, 3).reshape(bsz, seq, D_MODEL)
    return o @ w_o


def s5_layer(u, lam_re, lam_im, log_dt, b_re, b_im, c_re, c_im, d_skip, w_glu, b_glu):
    bsz, seq, _ = u.shape
    uf = u.astype(jnp.float32)
    ug = uf.reshape(bsz, seq, S5_GROUPS, S5_GROUP)
    dt = jnp.exp(log_dt.astype(jnp.float32))[:, None]
    lr, li = lam_re.astype(jnp.float32), lam_im.astype(jnp.float32)
    mag = jnp.exp(lr * dt)
    ar, ai = mag * jnp.cos(li * dt), mag * jnp.sin(li * dt)
    den = lr * lr + li * li
    er = ((ar - 1) * lr + ai * li) / den
    ei = (ai * lr - (ar - 1) * li) / den
    br, bi = b_re.astype(jnp.float32), b_im.astype(jnp.float32)
    bbr = er[..., None] * br - ei[..., None] * bi
    bbi = er[..., None] * bi + ei[..., None] * br
    bu_r = jnp.einsum('bsgc,gpc->bsgp', ug, bbr)
    bu_i = jnp.einsum('bsgc,gpc->bsgp', ug, bbi)
    a_r = jnp.broadcast_to(ar, (1, seq) + ar.shape)
    a_i = jnp.broadcast_to(ai, (1, seq) + ai.shape)

    def combine(e1, e2):
        a1r, a1i, b1r, b1i = e1
        a2r, a2i, b2r, b2i = e2
        return (a1r * a2r - a1i * a2i,
                a1r * a2i + a1i * a2r,
                a2r * b1r - a2i * b1i + b2r,
                a2r * b1i + a2i * b1r + b2i)

    _, _, xr, xi = lax.associative_scan(combine, (a_r, a_i, bu_r, bu_i), axis=1)
    y = (jnp.einsum('bsgp,gcp->bsgc', xr, c_re.astype(jnp.float32))
         - jnp.einsum('bsgp,gcp->bsgc', xi, c_im.astype(jnp.float32)))
    y = y.reshape(bsz, seq, D_MODEL) + d_skip.astype(jnp.float32) * uf
    y = jax.nn.gelu(y).astype(u.dtype)
    ga, gb = jnp.split(y @ w_glu + b_glu, 2, axis=-1)
    return ga * jax.nn.sigmoid(gb)


def conformer_conv(u, w_pw1, b_pw1, w_dw, b_dw, ln_g, ln_b, w_pw2, b_pw2):
    ga, gb = jnp.split(u @ w_pw1 + b_pw1, 2, axis=-1)
    h = ga * jax.nn.sigmoid(gb)
    hp = jnp.pad(h, ((0, 0), (CONV_WIDTH - 1, 0), (0, 0)))
    h = lax.conv_general_dilated(hp, w_dw[:, None, :], window_strides=(1,), padding='VALID',
                                 dimension_numbers=('NWC', 'WIO', 'NWC'),
                                 feature_group_count=D_MODEL) + b_dw
    hf = h.astype(jnp.float32)
    mu = jnp.mean(hf, axis=-1, keepdims=True)
    var = jnp.mean(jnp.square(hf - mu), axis=-1, keepdims=True)
    h = ((hf - mu) * lax.rsqrt(var + EPS) * ln_g.astype(jnp.float32)
         + ln_b.astype(jnp.float32)).astype(u.dtype)
    h = jax.nn.silu(h)
    return h @ w_pw2 + b_pw2


def swiglu(u, w_gate, w_up, w_down):
    return (jax.nn.silu(u @ w_gate) * (u @ w_up)) @ w_down


def setup_inputs(seed: int = 0) -> dict:
    key = jax.random.key(seed)
    keys = list(jax.random.split(key, 40))
    nk = lambda: keys.pop()
    nrm = lambda shape, std: jax.random.normal(nk(), shape, jnp.float32) * std
    D, F, G, P, GC = D_MODEL, D_FF, S5_GROUPS, S5_STATE, S5_GROUP
    NA, NB, NC = N_SB_LAYERS, N_S5_LAYERS, N_CV_LAYERS
    n = jnp.arange(P, dtype=jnp.float32)
    return {
        "x": nrm((BATCH, SEQ, D), 1.0),
        "c": nrm((BATCH, D), 1.0),
        "norm_g": 1.0 + nrm((DEPTH, 4, D), 0.05),
        "w_mod": nrm((DEPTH, D, 6 * D), 0.5 * D ** -0.5),
        "b_mod": nrm((DEPTH, 6 * D), 0.01),
        "sb_w_qkv": nrm((NA, D, 3 * D), D ** -0.5),
        "sb_w_o": nrm((NA, D, D), D ** -0.5),
        "s5_lam_re": -0.5 + nrm((NB, G, P), 0.01),
        "s5_lam_im": jnp.pi * n + nrm((NB, G, P), 0.01),
        "s5_log_dt": jax.random.uniform(nk(), (NB, G), jnp.float32,
                                        minval=math.log(S5_DT_MIN), maxval=math.log(S5_DT_MAX)),
        "s5_b_re": nrm((NB, G, P, GC), (2 * GC) ** -0.5),
        "s5_b_im": nrm((NB, G, P, GC), (2 * GC) ** -0.5),
        "s5_c_re": nrm((NB, G, GC, P), (2 * P) ** -0.5),
        "s5_c_im": nrm((NB, G, GC, P), (2 * P) ** -0.5),
        "s5_d": nrm((NB, D), 1.0),
        "s5_w_glu": nrm((NB, D, 2 * D), D ** -0.5),
        "s5_b_glu": nrm((NB, 2 * D), 0.01),
        "cv_w_pw1": nrm((NC, D, 2 * D), D ** -0.5),
        "cv_b_pw1": nrm((NC, 2 * D), 0.01),
        "cv_w_dw": nrm((NC, CONV_WIDTH, D), CONV_WIDTH ** -0.5),
        "cv_b_dw": nrm((NC, D), 0.01),
        "cv_ln_g": 1.0 + nrm((NC, D), 0.05),
        "cv_ln_b": nrm((NC, D), 0.01),
        "cv_w_pw2": nrm((NC, D, D), D ** -0.5),
        "cv_b_pw2": nrm((NC, D), 0.01),
        "ffn_w_gate": nrm((DEPTH, D, F), D ** -0.5),
        "ffn_w_up": nrm((DEPTH, D, F), D ** -0.5),
        "ffn_w_down": nrm((DEPTH, F, D), F ** -0.5),
    }


def reference(x, c, norm_g, w_mod, b_mod, sb_w_qkv, sb_w_o,
              s5_lam_re, s5_lam_im, s5_log_dt, s5_b_re, s5_b_im, s5_c_re, s5_c_im,
              s5_d, s5_w_glu, s5_b_glu,
              cv_w_pw1, cv_b_pw1, cv_w_dw, cv_b_dw, cv_ln_g, cv_ln_b, cv_w_pw2, cv_b_pw2,
              ffn_w_gate, ffn_w_up, ffn_w_down):
    mod_all = jnp.einsum('bd,lde->lbe', jax.nn.silu(c), w_mod) + b_mod[:, None, :]
    h = x
    for layer in range(DEPTH):
        sh_m, sc_m, g_m, sh_f, sc_f, g_f = jnp.split(mod_all[layer], 6, axis=-1)
        kind, j = layer % N_MIXERS, layer // N_MIXERS
        u = modulate(rms_norm(h, norm_g[layer, 0]), sh_m, sc_m)
        if kind == 0:
            m = stick_breaking_attention(u, sb_w_qkv[j], sb_w_o[j])
        elif kind == 1:
            m = s5_layer(u, s5_lam_re[j], s5_lam_im[j], s5_log_dt[j], s5_b_re[j], s5_b_im[j],
                         s5_c_re[j], s5_c_im[j], s5_d[j], s5_w_glu[j], s5_b_glu[j])
        else:
            m = conformer_conv(u, cv_w_pw1[j], cv_b_pw1[j], cv_w_dw[j], cv_b_dw[j],
                               cv_ln_g[j], cv_ln_b[j], cv_w_pw2[j], cv_b_pw2[j])
        h = h + g_m[:, None, :] * rms_norm(m, norm_g[layer, 1])
        u = modulate(rms_norm(h, norm_g[layer, 2]), sh_f, sc_f)
        f = swiglu(u, ffn_w_gate[layer], ffn_w_up[layer], ffn_w_down[layer])
        h = h + g_f[:, None, :] * rms_norm(f, norm_g[layer, 3])
    return h
```

```python
import functools

import numpy as np
import jax
import jax.numpy as jnp
from jax import lax
from jax.experimental import pallas as pl
from jax.experimental.pallas import tpu as pltpu

F32 = jnp.float32
BF16 = jnp.bfloat16

EPS = 1e-6
N_MIXERS = 3
SB_HEADS = 16
S5_GROUP = 16
S5_STATE = 64
CONV_WIDTH = 31

LANES = 128
SUBLANES = 8
VMEM_LIMIT = 56 * 1024 * 1024

TOK_TILE = 512
ATT_BLOCK = 128
ATT_CUTOFF = 110.0
S5_SEG = 64
S5_CHUNK = SUBLANES * S5_SEG
S5_LANE_GROUP = 512
CONV_HALO = 32


def _params(*sem):
    return pltpu.CompilerParams(dimension_semantics=sem, vmem_limit_bytes=VMEM_LIMIT)


def _const_spec(shape):
    nd = len(shape)
    return pl.BlockSpec(shape, lambda *_: (0,) * nd, pipeline_mode=pl.Buffered(1))


def _rms(x, g):
    ms = jnp.mean(x * x, axis=-1, keepdims=True)
    return x * lax.rsqrt(ms + EPS) * g


def _sigmoid(x):
    return 1.0 / (1.0 + jnp.exp(-x))


def _prenorm(x, ng_ref, mod_ref, norm_row, shift_row):
    y = _rms(x, ng_ref[norm_row:norm_row + 1, :])
    return (y * (1.0 + mod_ref[shift_row + 1:shift_row + 2, :])
            + mod_ref[shift_row:shift_row + 1, :])


def _mod_kernel(c_ref, w_ref, b_ref, o_ref):
    c = c_ref[...]
    o_ref[...] = jnp.dot(c * _sigmoid(c), w_ref[...],
                         preferred_element_type=F32) + b_ref[...]


def _modulation(c, w_mod, b_mod):
    depth, d, e = w_mod.shape
    bsz = c.shape[0]
    rows = SUBLANES
    c_pad = jnp.zeros((rows, d), F32).at[:bsz].set(c)
    tn = d
    out = pl.pallas_call(
        _mod_kernel,
        grid=(depth, e // tn),
        in_specs=[
            pl.BlockSpec((rows, d), lambda l, j: (0, 0)),
            pl.BlockSpec((None, d, tn), lambda l, j: (l, 0, j)),
            pl.BlockSpec((None, 1, tn), lambda l, j: (l, 0, j)),
        ],
        out_specs=pl.BlockSpec((None, rows, tn), lambda l, j: (l, 0, j)),
        out_shape=jax.ShapeDtypeStruct((depth, rows, e), F32),
        compiler_params=_params("arbitrary", "arbitrary"),
        name="adaln_mod",
    )(c_pad, w_mod, b_mod.reshape(depth, 1, e))
    return out[:, :bsz].reshape(depth, bsz, 6, d)


def _sb_qkv_kernel(h_ref, mod_ref, ng_ref, w_ref, o_ref):
    u = _prenorm(h_ref[...], ng_ref, mod_ref, 0, 0)
    o_ref[...] = jnp.dot(u.astype(BF16), w_ref[...],
                         preferred_element_type=F32).astype(BF16)


def _sb_qkv(h, mod_l, ng_l, w_qkv):
    bsz, seq, d = h.shape
    n = w_qkv.shape[1]
    tm = TOK_TILE
    return pl.pallas_call(
        _sb_qkv_kernel,
        grid=(bsz, seq // tm),
        in_specs=[
            pl.BlockSpec((None, tm, d), lambda b, i: (b, i, 0)),
            pl.BlockSpec((None, 6, d), lambda b, i: (b, 0, 0)),
            _const_spec((4, d)),
            _const_spec((d, n)),
        ],
        out_specs=pl.BlockSpec((None, tm, n), lambda b, i: (b, i, 0)),
        out_shape=jax.ShapeDtypeStruct((bsz, seq, n), BF16),
        compiler_params=_params("arbitrary", "arbitrary"),
        name="sb_qkv",
    )(h, mod_l, ng_l, w_qkv)


def _split3(x):
    hi = x.astype(BF16)
    r = x - hi.astype(F32)
    mid = r.astype(BF16)
    lo = (r - mid.astype(F32)).astype(BF16)
    return hi, mid, lo


def _sb_attn_kernel(q_ref, k_ref, v_ref, o_ref, acc_ref, car_ref, *, scale):
    qb = kb_size = ATT_BLOCK
    hd = LANES // 2
    qi = pl.program_id(2)
    acc_ref[...] = jnp.zeros_like(acc_ref)
    car_ref[...] = jnp.zeros_like(car_ref)
    q = q_ref[...].astype(F32)
    lane = lax.broadcasted_iota(jnp.int32, (qb, LANES), 1)
    q_heads = (jnp.where(lane < hd, q, 0.0).astype(BF16),
               jnp.where(lane >= hd, q, 0.0).astype(BF16))
    row = lax.broadcasted_iota(jnp.int32, (qb, kb_size), 0)
    col = lax.broadcasted_iota(jnp.int32, (qb, kb_size), 1)
    jj = lax.broadcasted_iota(jnp.int32, (kb_size, 2 * kb_size), 0)
    ss = lax.broadcasted_iota(jnp.int32, (kb_size, 2 * kb_size), 1)
    suffix_and_total = jnp.where((ss >= kb_size) | (jj > ss), 1.0, 0.0).astype(BF16)

    def cond(carry):
        kb, top = carry
        return jnp.logical_and(kb >= 0, top > -ATT_CUTOFF)

    def body(carry):
        kb, _ = carry
        start = pl.multiple_of(kb * kb_size, kb_size)
        k_blk = k_ref[pl.ds(start, kb_size), :]
        v_blk = v_ref[pl.ds(start, kb_size), :]
        mask = (col + kb * kb_size) < (row + qi * qb)
        top = jnp.float32(-jnp.inf)
        for h in range(2):
            z = lax.dot_general(q_heads[h], k_blk, (((1,), (1,)), ((), ())),
                                preferred_element_type=F32) * scale
            sp = jnp.log1p(jnp.exp(-jnp.abs(z)))
            log_beta = jnp.minimum(z, 0.0) - sp
            log_keep = jnp.where(mask, jnp.minimum(-z, 0.0) - sp, 0.0)
            hi, mid, lo = _split3(log_keep)
            sums = (jnp.dot(hi, suffix_and_total, preferred_element_type=F32)
                    + jnp.dot(mid, suffix_and_total, preferred_element_type=F32)
                    + jnp.dot(lo, suffix_and_total, preferred_element_type=F32))
            car = car_ref[h]
            log_later = sums[:, :kb_size] + car
            w = jnp.where(mask, jnp.exp(log_beta + log_later), 0.0)
            acc_ref[h] += jnp.dot(w.astype(BF16), v_blk, preferred_element_type=F32)
            car = car + sums[:, kb_size:]
            car_ref[h] = car
            top = jnp.maximum(top, jnp.max(car))
        return kb - 1, top

    lax.while_loop(cond, body, (qi, jnp.float32(0.0)))
    o_ref[...] = jnp.where(lane < hd, acc_ref[0], acc_ref[1]).astype(BF16)


def _sb_attention(qkv, d):
    bsz, seq, _ = qkv.shape
    qb = ATT_BLOCK
    head_dim = d // SB_HEADS
    assert 2 * head_dim == LANES
    pairs = d // LANES
    kern = functools.partial(_sb_attn_kernel, scale=head_dim ** -0.5)
    return pl.pallas_call(
        kern,
        grid=(bsz, pairs, seq // qb),
        in_specs=[
            pl.BlockSpec((None, qb, LANES), lambda b, p, i: (b, i, p)),
            pl.BlockSpec((None, seq, LANES), lambda b, p, i: (b, 0, pairs + p)),
            pl.BlockSpec((None, seq, LANES), lambda b, p, i: (b, 0, 2 * pairs + p)),
        ],
        out_specs=pl.BlockSpec((None, qb, LANES), lambda b, p, i: (b, i, p)),
        out_shape=jax.ShapeDtypeStruct((bsz, seq, d), BF16),
        scratch_shapes=[pltpu.VMEM((2, qb, LANES), F32),
                        pltpu.VMEM((2, qb, ATT_BLOCK), F32)],
        compiler_params=_params("arbitrary", "arbitrary", "arbitrary"),
        name="sb_attention",
    )(qkv, qkv, qkv)


def _resid_ffn_kernel(*refs, has_wo, ff_chunk):
    if has_wo:
        h_ref, m_ref, mod_ref, ng_ref, wo_ref, wg_ref, wu_ref, wd_ref, o_ref = refs
        m = jnp.dot(m_ref[...], wo_ref[...], preferred_element_type=F32)
    else:
        h_ref, m_ref, mod_ref, ng_ref, wg_ref, wu_ref, wd_ref, o_ref = refs
        m = m_ref[...]
    h1 = h_ref[...] + mod_ref[2:3, :] * _rms(m, ng_ref[1:2, :])
    u = _prenorm(h1, ng_ref, mod_ref, 2, 3).astype(BF16)
    d_ff = wg_ref.shape[1]
    acc = jnp.zeros_like(h1)
    for c in range(d_ff // ff_chunk):
        lo, hi = c * ff_chunk, (c + 1) * ff_chunk
        g = jnp.dot(u, wg_ref[:, lo:hi], preferred_element_type=F32)
        up = jnp.dot(u, wu_ref[:, lo:hi], preferred_element_type=F32)
        a = (g * _sigmoid(g) * up).astype(BF16)
        acc = acc + jnp.dot(a, wd_ref[lo:hi, :], preferred_element_type=F32)
    o_ref[...] = h1 + mod_ref[5:6, :] * _rms(acc, ng_ref[3:4, :])


def _resid_ffn(h, m, mod_l, ng_l, w_o, w_gate, w_up, w_down):
    bsz, seq, d = h.shape
    d_ff = w_gate.shape[1]
    tm = TOK_TILE
    ff_chunk = d_ff // 2
    assert ff_chunk % LANES == 0
    has_wo = w_o is not None
    tile = lambda: pl.BlockSpec((None, tm, d), lambda b, i: (b, i, 0))
    in_specs = [tile(), tile(),
                pl.BlockSpec((None, 6, d), lambda b, i: (b, 0, 0)),
                _const_spec((4, d))]
    args = [h, m, mod_l, ng_l]
    if has_wo:
        in_specs.append(_const_spec((d, d)))
        args.append(w_o)
    in_specs += [_const_spec((d, d_ff)), _const_spec((d, d_ff)), _const_spec((d_ff, d))]
    args += [w_gate, w_up, w_down]
    return pl.pallas_call(
        functools.partial(_resid_ffn_kernel, has_wo=has_wo, ff_chunk=ff_chunk),
        grid=(bsz, seq // tm),
        in_specs=in_specs,
        out_specs=tile(),
        out_shape=jax.ShapeDtypeStruct((bsz, seq, d), F32),
        compiler_params=_params("arbitrary", "arbitrary"),
        name="resid_ffn_wo" if has_wo else "resid_ffn",
    )(*args)


def _s5_param_kernel(lr_ref, li_ref, ldt_ref, br_ref, bi_ref,
                     bbr_ref, bbi_ref, pow_ref):
    n = lr_ref.shape[1]
    lr, li = lr_ref[...], li_ref[...]
    dt = jnp.exp(ldt_ref[...])
    mag = jnp.exp(lr * dt)
    ar, ai = mag * jnp.cos(li * dt), mag * jnp.sin(li * dt)
    den = lr * lr + li * li
    er = ((ar - 1.0) * lr + ai * li) / den
    ei = (ai * lr - (ar - 1.0) * li) / den
    br, bi = br_ref[...], bi_ref[...]
    bbr_ref[...] = er * br - ei * bi
    bbi_ref[...] = er * bi + ei * br
    pr, pi = ar, ai
    for k in range(pow_ref.shape[0]):
        pow_ref[k:k + 1, 0:n] = pr
        pow_ref[k:k + 1, n:2 * n] = pi
        pr, pi = pr * ar - pi * ai, pr * ai + pi * ar


def _s5_kernel(h_ref, mod_ref, ng_ref, perm_ref, permt_ref, wbu_ref, pow_ref,
               wcr_ref, wci_ref, dskip_ref, wglu_ref, bglu_ref, o_ref,
               x_ref, yp_ref, seg_ref, carry_ref):
    lc, d = h_ref.shape
    ns = x_ref.shape[1] // 2
    seg = lc // SUBLANES
    lg = S5_LANE_GROUP

    @pl.when(pl.program_id(1) == 0)
    def _():
        carry_ref[...] = jnp.zeros_like(carry_ref)

    u = _prenorm(h_ref[...], ng_ref, mod_ref, 0, 0)
    u_perm = jnp.dot(perm_ref[...], u.astype(BF16),
                     preferred_element_type=F32).astype(BF16)

    n_blocks = wbu_ref.shape[0]
    blocks_per_part = n_blocks // 2
    for cb in range(n_blocks):
        slab = (cb % blocks_per_part) // 2
        x_ref[:, cb * 256:(cb + 1) * 256] = jnp.dot(
            u_perm[:, slab * LANES:(slab + 1) * LANES], wbu_ref[cb],
            preferred_element_type=F32)

    for g in range(ns // lg):
        re = slice(g * lg, (g + 1) * lg)
        im = slice(ns + g * lg, ns + (g + 1) * lg)
        ar = jnp.broadcast_to(pow_ref[0:1, re], (SUBLANES, lg))
        ai = jnp.broadcast_to(pow_ref[0:1, im], (SUBLANES, lg))

        def scan_step(k, carry, re=re, im=im, ar=ar, ai=ai):
            xr, xi = carry
            rows = pl.ds(pl.multiple_of(k * SUBLANES, SUBLANES), SUBLANES)
            nr = ar * xr - ai * xi + x_ref[rows, re]
            ni = ar * xi + ai * xr + x_ref[rows, im]
            x_ref[rows, re] = nr
            x_ref[rows, im] = ni
            return nr, ni

        zero = jnp.zeros((SUBLANES, lg), F32)
        lax.fori_loop(0, seg, scan_step, (zero, zero))

    last = (seg - 1) * SUBLANES
    apr, api = pow_ref[seg - 1:seg, 0:ns], pow_ref[seg - 1:seg, ns:2 * ns]
    sr, si = carry_ref[0:1, 0:ns], carry_ref[0:1, ns:2 * ns]
    for j in range(SUBLANES):
        seg_ref[j:j + 1, 0:ns] = sr
        seg_ref[j:j + 1, ns:2 * ns] = si
        er = x_ref[last + j:last + j + 1, 0:ns]
        ei = x_ref[last + j:last + j + 1, ns:2 * ns]
        sr, si = apr * sr - api * si + er, apr * si + api * sr + ei
    carry_ref[0:1, 0:ns] = sr
    carry_ref[0:1, ns:2 * ns] = si

    for g in range(ns // lg):
        re = slice(g * lg, (g + 1) * lg)
        im = slice(ns + g * lg, ns + (g + 1) * lg)
        s_r, s_i = seg_ref[:, re], seg_ref[:, im]

        def fix_step(k, carry, re=re, im=im, s_r=s_r, s_i=s_i):
            rows = pl.ds(pl.multiple_of(k * SUBLANES, SUBLANES), SUBLANES)
            pr = jnp.broadcast_to(pow_ref[pl.ds(k, 1), re], (SUBLANES, lg))
            pi = jnp.broadcast_to(pow_ref[pl.ds(k, 1), im], (SUBLANES, lg))
            x_ref[rows, re] = x_ref[rows, re] + pr * s_r - pi * s_i
            x_ref[rows, im] = x_ref[rows, im] + pr * s_i + pi * s_r
            return carry

        lax.fori_loop(0, seg, fix_step, 0)

    n_out = wcr_ref.shape[0]
    kw = ns // n_out
    for ob in range(n_out):
        xr = x_ref[:, ob * kw:(ob + 1) * kw].astype(BF16)
        xi = x_ref[:, ns + ob * kw:ns + (ob + 1) * kw].astype(BF16)
        yp_ref[:, ob * LANES:(ob + 1) * LANES] = (
            jnp.dot(xr, wcr_ref[ob], preferred_element_type=F32)
            + jnp.dot(xi, wci_ref[ob], preferred_element_type=F32))

    hi, mid, lo = _split3(yp_ref[...])
    pt = permt_ref[...]
    y = (jnp.dot(pt, hi, preferred_element_type=F32)
         + jnp.dot(pt, mid, preferred_element_type=F32)
         + jnp.dot(pt, lo, preferred_element_type=F32))
    y = y + dskip_ref[...] * u
    y = 0.5 * y * (1.0 + jnp.tanh(0.7978845608028654 * (y + 0.044715 * (y * y * y))))
    z = jnp.dot(y.astype(BF16), wglu_ref[...],
                preferred_element_type=F32) + bglu_ref[...]
    o_ref[...] = z[:, :d] * _sigmoid(z[:, d:])


def _s5_mixer(h, mod_l, ng_l, lam_re, lam_im, log_dt, b_re, b_im, c_re, c_im,
              d_skip, w_glu, b_glu):
    bsz, seq, d = h.shape
    g, p = lam_re.shape
    gc = b_re.shape[2]
    ns = g * p
    lc, seg = S5_CHUNK, S5_SEG
    assert gc == S5_GROUP and p == S5_STATE and g * gc == d

    flat = lambda a: a.reshape(1, ns)
    chan_major = lambda a: a.transpose(2, 0, 1).reshape(gc, ns)
    bbr, bbi, powers = pl.pallas_call(
        _s5_param_kernel,
        out_shape=(jax.ShapeDtypeStruct((gc, ns), F32),
                   jax.ShapeDtypeStruct((gc, ns), F32),
                   jax.ShapeDtypeStruct((seg, 2 * ns), F32)),
        compiler_params=pltpu.CompilerParams(vmem_limit_bytes=VMEM_LIMIT),
        name="s5_params",
    )(flat(lam_re), flat(lam_im), flat(jnp.repeat(log_dt, p)),
      chan_major(b_re), chan_major(b_im))

    def bu_blocks(bb):
        t = bb.reshape(gc, ns // 256, 4, p)
        t = jnp.einsum('cbgp,gh->bhcgp', t, jnp.eye(4, dtype=F32))
        t = t.reshape(ns // 256, 4 * gc, 256)
        half = jnp.eye(2, dtype=F32)[jnp.arange(ns // 256) % 2]
        return jnp.einsum('bkn,bq->bqkn', t, half).reshape(ns // 256, LANES, 256)

    wbu = jnp.concatenate([bu_blocks(bbr), bu_blocks(bbi)], axis=0).astype(BF16)

    def c_blocks(cc):
        t = cc.reshape(d // LANES, 8, gc, p)
        t = jnp.einsum('ogcp,gh->ogphc', t, jnp.eye(8, dtype=F32))
        return t.reshape(d // LANES, 8 * p, LANES).astype(BF16)

    wcr, wci = c_blocks(c_re), c_blocks(-c_im)

    r = np.arange(lc)
    perm_np = np.zeros((lc, lc), np.float32)
    perm_np[r, (r % SUBLANES) * seg + r // SUBLANES] = 1.0
    perm = jnp.asarray(perm_np, BF16)
    perm_t = jnp.asarray(perm_np.T, BF16)

    kw = ns // (d // LANES)
    return pl.pallas_call(
        _s5_kernel,
        grid=(bsz, seq // lc),
        in_specs=[
            pl.BlockSpec((None, lc, d), lambda b, i: (b, i, 0)),
            pl.BlockSpec((None, 6, d), lambda b, i: (b, 0, 0)),
            _const_spec((4, d)),
            _const_spec((lc, lc)),
            _const_spec((lc, lc)),
            _const_spec((2 * ns // 256, LANES, 256)),
            _const_spec((seg, 2 * ns)),
            _const_spec((d // LANES, kw, LANES)),
            _const_spec((d // LANES, kw, LANES)),
            _const_spec((1, d)),
            _const_spec((d, 2 * d)),
            _const_spec((1, 2 * d)),
        ],
        out_specs=pl.BlockSpec((None, lc, d), lambda b, i: (b, i, 0)),
        out_shape=jax.ShapeDtypeStruct((bsz, seq, d), F32),
        scratch_shapes=[pltpu.VMEM((lc, 2 * ns), F32),
                        pltpu.VMEM((lc, d), F32),
                        pltpu.VMEM((SUBLANES, 2 * ns), F32),
                        pltpu.VMEM((1, 2 * ns), F32)],
        compiler_params=_params("arbitrary", "arbitrary"),
        name="s5_mixer",
    )(h, mod_l, ng_l, perm, perm_t, wbu, powers, wcr, wci,
      d_skip.reshape(1, d), w_glu.astype(BF16), b_glu.reshape(1, 2 * d))


def _conv_kernel(h_ref, mod_ref, ng_ref, w1_ref, b1_ref, wdw_ref, bdw_ref,
                 lng_ref, lnb_ref, w2_ref, b2_ref, o_ref, buf_ref):
    tm, d = h_ref.shape
    halo = CONV_HALO

    @pl.when(pl.program_id(1) == 0)
    def _():
        buf_ref[0:halo, :] = jnp.zeros((halo, d), F32)

    u = _prenorm(h_ref[...], ng_ref, mod_ref, 0, 0)
    z = jnp.dot(u.astype(BF16), w1_ref[...],
                preferred_element_type=F32) + b1_ref[...]
    buf_ref[halo:halo + tm, :] = z[:, :d] * _sigmoid(z[:, d:])

    width = wdw_ref.shape[0]
    first = halo - (width - 1)
    acc = jnp.zeros((tm, d), F32) + bdw_ref[...]
    for j in range(width):
        acc = acc + wdw_ref[j:j + 1, :] * buf_ref[first + j:first + j + tm, :]
    buf_ref[0:halo, :] = buf_ref[tm:tm + halo, :]

    mu = jnp.mean(acc, axis=-1, keepdims=True)
    cen = acc - mu
    var = jnp.mean(cen * cen, axis=-1, keepdims=True)
    y = cen * lax.rsqrt(var + EPS) * lng_ref[...] + lnb_ref[...]
    y = y * _sigmoid(y)
    o_ref[...] = jnp.dot(y.astype(BF16), w2_ref[...],
                         preferred_element_type=F32) + b2_ref[...]


def _conv_mixer(h, mod_l, ng_l, w_pw1, b_pw1, w_dw, b_dw, ln_g, ln_b, w_pw2, b_pw2):
    bsz, seq, d = h.shape
    tm = TOK_TILE
    width = w_dw.shape[0]
    assert width - 1 <= CONV_HALO <= tm
    row = lambda a: a.reshape(1, -1)
    return pl.pallas_call(
        _conv_kernel,
        grid=(bsz, seq // tm),
        in_specs=[
            pl.BlockSpec((None, tm, d), lambda b, i: (b, i, 0)),
            pl.BlockSpec((None, 6, d), lambda b, i: (b, 0, 0)),
            _const_spec((4, d)),
            _const_spec((d, 2 * d)),
            _const_spec((1, 2 * d)),
            _const_spec((width, d)),
            _const_spec((1, d)),
            _const_spec((1, d)),
            _const_spec((1, d)),
            _const_spec((d, d)),
            _const_spec((1, d)),
        ],
        out_specs=pl.BlockSpec((None, tm, d), lambda b, i: (b, i, 0)),
        out_shape=jax.ShapeDtypeStruct((bsz, seq, d), F32),
        scratch_shapes=[pltpu.VMEM((CONV_HALO + tm, d), F32)],
        compiler_params=_params("arbitrary", "arbitrary"),
        name="conformer_conv",
    )(h, mod_l, ng_l, w_pw1.astype(BF16), row(b_pw1), w_dw, row(b_dw),
      row(ln_g), row(ln_b), w_pw2.astype(BF16), row(b_pw2))


def kernel(x, c, norm_g, w_mod, b_mod, sb_w_qkv, sb_w_o, s5_lam_re, s5_lam_im, s5_log_dt, s5_b_re, s5_b_im, s5_c_re, s5_c_im, s5_d, s5_w_glu, s5_b_glu, cv_w_pw1, cv_b_pw1, cv_w_dw, cv_b_dw, cv_ln_g, cv_ln_b, cv_w_pw2, cv_b_pw2, ffn_w_gate, ffn_w_up, ffn_w_down):
    depth = norm_g.shape[0]
    d = x.shape[-1]
    mod_all = _modulation(c, w_mod, b_mod)
    h = x
    for layer in range(depth):
        kind, j = layer % N_MIXERS, layer // N_MIXERS
        mod_l, ng_l = mod_all[layer], norm_g[layer]
        w_o = None
        if kind == 0:
            qkv = _sb_qkv(h, mod_l, ng_l, sb_w_qkv[j].astype(BF16))
            m = _sb_attention(qkv, d)
            w_o = sb_w_o[j].astype(BF16)
        elif kind == 1:
            m = _s5_mixer(h, mod_l, ng_l, s5_lam_re[j], s5_lam_im[j], s5_log_dt[j],
                          s5_b_re[j], s5_b_im[j], s5_c_re[j], s5_c_im[j], s5_d[j],
                          s5_w_glu[j], s5_b_glu[j])
        else:
            m = _conv_mixer(h, mod_l, ng_l, cv_w_pw1[j], cv_b_pw1[j], cv_w_dw[j],
                            cv_b_dw[j], cv_ln_g[j], cv_ln_b[j], cv_w_pw2[j], cv_b_pw2[j])
        h = _resid_ffn(h, m, mod_l, ng_l, w_o, ffn_w_gate[layer].astype(BF16),
                       ffn_w_up[layer].astype(BF16), ffn_w_down[layer].astype(BF16))
    return h
```

```python
import functools

import numpy as np
import jax
import jax.numpy as jnp
from jax import lax
from jax.experimental import pallas as pl
from jax.experimental.pallas import tpu as pltpu

F32 = jnp.float32
BF16 = jnp.bfloat16

EPS = 1e-6
N_MIXERS = 3
SB_HEADS = 16
S5_GROUP = 16
S5_STATE = 64
CONV_WIDTH = 31

LANES = 128
SUBLANES = 8
VMEM_LIMIT = 56 * 1024 * 1024

TOK_TILE = 512
ATT_Q = 128
ATT_PAST = 256
ATT_PAD = 2 * ATT_PAST
ATT_HEAD_PAIRS = 2
ATT_CUTOFF = 110.0
S5_SEG = 64
S5_CHUNK = SUBLANES * S5_SEG
S5_LANE_GROUP = 1024
CONV_HALO = 32
CONV_ROW_GROUP = 32


def _params(*sem):
    return pltpu.CompilerParams(dimension_semantics=sem, vmem_limit_bytes=VMEM_LIMIT)


def _const_spec(shape):
    nd = len(shape)
    return pl.BlockSpec(shape, lambda *_: (0,) * nd, pipeline_mode=pl.Buffered(1))


def _rms(x, g):
    ms = jnp.mean(x * x, axis=-1, keepdims=True)
    return x * lax.rsqrt(ms + EPS) * g


def _sigmoid(x):
    return 1.0 / (1.0 + jnp.exp(-x))


def _prenorm(x, ng_ref, mod_ref, norm_row, shift_row):
    y = _rms(x, ng_ref[norm_row:norm_row + 1, :])
    return (y * (1.0 + mod_ref[shift_row + 1:shift_row + 2, :])
            + mod_ref[shift_row:shift_row + 1, :])


def _mod_kernel(c_ref, w_ref, b_ref, o_ref):
    c = c_ref[...]
    o_ref[...] = jnp.dot(c * _sigmoid(c), w_ref[...],
                         preferred_element_type=F32) + b_ref[...]


def _modulation(c, w_mod, b_mod):
    depth, d, e = w_mod.shape
    bsz = c.shape[0]
    rows = SUBLANES
    c_pad = jnp.zeros((rows, d), F32).at[:bsz].set(c)
    tn = d
    out = pl.pallas_call(
        _mod_kernel,
        grid=(depth, e // tn),
        in_specs=[
            pl.BlockSpec((rows, d), lambda l, j: (0, 0)),
            pl.BlockSpec((None, d, tn), lambda l, j: (l, 0, j)),
            pl.BlockSpec((None, 1, tn), lambda l, j: (l, 0, j)),
        ],
        out_specs=pl.BlockSpec((None, rows, tn), lambda l, j: (l, 0, j)),
        out_shape=jax.ShapeDtypeStruct((depth, rows, e), F32),
        compiler_params=_params("arbitrary", "arbitrary"),
        name="adaln_mod",
    )(c_pad, w_mod, b_mod.reshape(depth, 1, e))
    return out[:, :bsz].reshape(depth, bsz, 6, d)


def _sb_qkv_kernel(h_ref, mod_ref, ng_ref, w_ref, o_ref):
    @pl.when(pl.program_id(1) == 0)
    def _():
        o_ref[...] = jnp.zeros_like(o_ref)

    @pl.when(pl.program_id(1) > 0)
    def _():
        u = _prenorm(h_ref[...], ng_ref, mod_ref, 0, 0)
        o_ref[...] = jnp.dot(u.astype(BF16), w_ref[...],
                             preferred_element_type=F32).astype(BF16)


def _sb_qkv(h, mod_l, ng_l, w_qkv):
    bsz, seq, d = h.shape
    n = w_qkv.shape[1]
    tm = ATT_PAD
    return pl.pallas_call(
        _sb_qkv_kernel,
        grid=(bsz, seq // tm + 1),
        in_specs=[
            pl.BlockSpec((None, tm, d), lambda b, i: (b, jnp.maximum(i - 1, 0), 0)),
            pl.BlockSpec((None, 6, d), lambda b, i: (b, 0, 0)),
            _const_spec((4, d)),
            _const_spec((d, n)),
        ],
        out_specs=pl.BlockSpec((None, tm, n), lambda b, i: (b, i, 0)),
        out_shape=jax.ShapeDtypeStruct((bsz, seq + tm, n), BF16),
        compiler_params=_params("arbitrary", "arbitrary"),
        name="sb_qkv",
    )(h, mod_l, ng_l, w_qkv)


def _split2(x):
    hi = x.astype(BF16)
    return hi, (x - hi.astype(F32)).astype(BF16)


def _split3(x):
    hi = x.astype(BF16)
    r = x - hi.astype(F32)
    mid = r.astype(BF16)
    lo = (r - mid.astype(F32)).astype(BF16)
    return hi, mid, lo


def _suffix_sums(x, st_ref):
    rows = x.shape[0]
    hi, lo = _split2(x)
    both = jnp.dot(jnp.concatenate([hi, lo], axis=0), st_ref[...],
                   preferred_element_type=F32)
    return both[:rows] + both[rows:]


def _sb_attn_kernel(q_ref, k_ref, v_ref, st_past_ref, st_diag_ref, o_ref,
                    acc_ref, car_ref, *, scale):
    qb, past = ATT_Q, ATT_PAST
    hd = LANES // 2
    pairs = q_ref.shape[1] // LANES
    qi = pl.program_id(2)
    q0 = pl.multiple_of(qi * qb, qb)
    q = q_ref[...].astype(F32) * scale
    lane = lax.broadcasted_iota(jnp.int32, (qb, LANES), 1)
    q_pairs = []
    for p in range(pairs):
        qp = q[:, p * LANES:(p + 1) * LANES]
        q_pairs.append(jnp.concatenate(
            [jnp.where(lane < hd, qp, 0.0), jnp.where(lane >= hd, qp, 0.0)],
            axis=0).astype(BF16))
    rows = 2 * pairs * qb
    row = lax.broadcasted_iota(jnp.int32, (rows, qb), 0) & (qb - 1)
    col = lax.broadcasted_iota(jnp.int32, (rows, qb), 1)
    causal = col < row

    def softplus_terms(z):
        sp = jnp.log(1.0 + jnp.exp(-jnp.abs(z)))
        return jnp.minimum(z, 0.0) - sp, jnp.maximum(z, 0.0) + sp

    def scores(k_win):
        return jnp.concatenate(
            [lax.dot_general(q_pairs[p], k_win[:, p * LANES:(p + 1) * LANES],
                             (((1,), (1,)), ((), ())), preferred_element_type=F32)
             for p in range(pairs)], axis=0)

    def values(w, v_win):
        wb = w.astype(BF16)
        return jnp.concatenate(
            [jnp.dot(wb[p * 2 * qb:(p + 1) * 2 * qb], v_win[:, p * LANES:(p + 1) * LANES],
                     preferred_element_type=F32) for p in range(pairs)], axis=0)

    start = pl.multiple_of(q0 + (ATT_PAD - past), qb)
    log_beta, neg_keep = softplus_terms(scores(k_ref[pl.ds(start, past + qb), :]))
    sums_d = _suffix_sums(jnp.where(causal, neg_keep[:, past:], 0.0), st_diag_ref)
    later_d, total_d = sums_d[:, :qb], sums_d[:, qb:]
    sums_p = _suffix_sums(neg_keep[:, :past], st_past_ref)
    later_p = sums_p[:, :past] + jnp.concatenate([total_d] * (past // LANES), axis=1)
    w = jnp.concatenate(
        [jnp.exp(log_beta[:, :past] - later_p),
         jnp.where(causal, jnp.exp(log_beta[:, past:] - later_d), 0.0)], axis=1)
    acc_ref[...] = values(w, v_ref[pl.ds(start, past + qb), :])
    car = total_d + sums_p[:, past:]
    car_ref[...] = car

    def cond(carry):
        ws, lowest = carry
        return jnp.logical_and(ws > ATT_PAD - past, lowest < ATT_CUTOFF)

    def body(carry):
        ws, _ = carry
        ws = pl.multiple_of(ws, qb)
        log_beta, neg_keep = softplus_terms(scores(k_ref[pl.ds(ws, past), :]))
        sums = _suffix_sums(neg_keep, st_past_ref)
        car = car_ref[...]
        later = sums[:, :past] + jnp.concatenate([car] * (past // LANES), axis=1)
        acc_ref[...] += values(jnp.exp(log_beta - later), v_ref[pl.ds(ws, past), :])
        car = car + sums[:, past:]
        car_ref[...] = car
        return ws - past, jnp.min(car)

    lax.while_loop(cond, body, (q0 + (ATT_PAD - 2 * past), jnp.min(car)))
    for p in range(pairs):
        o_ref[:, p * LANES:(p + 1) * LANES] = jnp.where(
            lane < hd, acc_ref[2 * p * qb:(2 * p + 1) * qb, :],
            acc_ref[(2 * p + 1) * qb:(2 * p + 2) * qb, :]).astype(BF16)


def _suffix_matrix(width):
    j = np.arange(width)[:, None]
    s = np.arange(width + LANES)[None, :]
    return jnp.asarray(((s >= width) | (j > s)).astype(np.float32), BF16)


def _sb_attention(qkv, d):
    bsz, padded, _ = qkv.shape
    seq = padded - ATT_PAD
    qb, past = ATT_Q, ATT_PAST
    head_dim = d // SB_HEADS
    assert 2 * head_dim == LANES and ATT_PAD == 2 * past and past % qb == 0
    width = ATT_HEAD_PAIRS * LANES
    groups = d // width
    kern = functools.partial(_sb_attn_kernel, scale=head_dim ** -0.5)
    return pl.pallas_call(
        kern,
        grid=(bsz, groups, seq // qb),
        in_specs=[
            pl.BlockSpec((None, qb, width), lambda b, p, i: (b, i + ATT_PAD // qb, p)),
            pl.BlockSpec((None, padded, width), lambda b, p, i: (b, 0, groups + p)),
            pl.BlockSpec((None, padded, width), lambda b, p, i: (b, 0, 2 * groups + p)),
            _const_spec((past, past + LANES)),
            _const_spec((qb, qb + LANES)),
        ],
        out_specs=pl.BlockSpec((None, qb, width), lambda b, p, i: (b, i, p)),
        out_shape=jax.ShapeDtypeStruct((bsz, seq, d), BF16),
        scratch_shapes=[pltpu.VMEM((2 * ATT_HEAD_PAIRS * qb, LANES), F32),
                        pltpu.VMEM((2 * ATT_HEAD_PAIRS * qb, LANES), F32)],
        compiler_params=_params("arbitrary", "arbitrary", "arbitrary"),
        name="sb_attention",
    )(qkv, qkv, qkv, _suffix_matrix(past), _suffix_matrix(qb))


def _resid_ffn_kernel(*refs, has_wo, ff_chunk):
    if has_wo:
        h_ref, m_ref, mod_ref, ng_ref, wo_ref, wg_ref, wu_ref, wd_ref, o_ref = refs
        m = jnp.dot(m_ref[...], wo_ref[...], preferred_element_type=F32)
    else:
        h_ref, m_ref, mod_ref, ng_ref, wg_ref, wu_ref, wd_ref, o_ref = refs
        m = m_ref[...]
    h1 = h_ref[...] + mod_ref[2:3, :] * _rms(m, ng_ref[1:2, :])
    u = _prenorm(h1, ng_ref, mod_ref, 2, 3).astype(BF16)
    d_ff = wg_ref.shape[1]
    acc = jnp.zeros_like(h1)
    for c in range(d_ff // ff_chunk):
        lo, hi = c * ff_chunk, (c + 1) * ff_chunk
        g = jnp.dot(u, wg_ref[:, lo:hi], preferred_element_type=F32)
        up = jnp.dot(u, wu_ref[:, lo:hi], preferred_element_type=F32)
        a = (g * _sigmoid(g) * up).astype(BF16)
        acc = acc + jnp.dot(a, wd_ref[lo:hi, :], preferred_element_type=F32)
    o_ref[...] = h1 + mod_ref[5:6, :] * _rms(acc, ng_ref[3:4, :])


def _resid_ffn(h, m, mod_l, ng_l, w_o, w_gate, w_up, w_down):
    bsz, seq, d = h.shape
    d_ff = w_gate.shape[1]
    tm = TOK_TILE
    ff_chunk = d_ff // 2
    assert ff_chunk % LANES == 0
    has_wo = w_o is not None
    tile = lambda: pl.BlockSpec((None, tm, d), lambda b, i: (b, i, 0))
    in_specs = [tile(), tile(),
                pl.BlockSpec((None, 6, d), lambda b, i: (b, 0, 0)),
                _const_spec((4, d))]
    args = [h, m, mod_l, ng_l]
    if has_wo:
        in_specs.append(_const_spec((d, d)))
        args.append(w_o)
    in_specs += [_const_spec((d, d_ff)), _const_spec((d, d_ff)), _const_spec((d_ff, d))]
    args += [w_gate, w_up, w_down]
    return pl.pallas_call(
        functools.partial(_resid_ffn_kernel, has_wo=has_wo, ff_chunk=ff_chunk),
        grid=(bsz, seq // tm),
        in_specs=in_specs,
        out_specs=tile(),
        out_shape=jax.ShapeDtypeStruct((bsz, seq, d), F32),
        compiler_params=_params("arbitrary", "arbitrary"),
        name="resid_ffn_wo" if has_wo else "resid_ffn",
    )(*args)


def _s5_param_kernel(lr_ref, li_ref, ldt_ref, br_ref, bi_ref,
                     bbr_ref, bbi_ref, pow_ref):
    n = lr_ref.shape[1]
    lr, li = lr_ref[...], li_ref[...]
    dt = jnp.exp(ldt_ref[...])
    mag = jnp.exp(lr * dt)
    ar, ai = mag * jnp.cos(li * dt), mag * jnp.sin(li * dt)
    den = lr * lr + li * li
    er = ((ar - 1.0) * lr + ai * li) / den
    ei = (ai * lr - (ar - 1.0) * li) / den
    br, bi = br_ref[...], bi_ref[...]
    bbr_ref[...] = er * br - ei * bi
    bbi_ref[...] = er * bi + ei * br
    pr, pi = ar, ai
    for k in range(pow_ref.shape[0]):
        pow_ref[k:k + 1, 0:n] = pr
        pow_ref[k:k + 1, n:2 * n] = pi
        pr, pi = pr * ar - pi * ai, pr * ai + pi * ar


def _s5_kernel(h_ref, mod_ref, ng_ref, perm_ref, permt_ref, wbu_ref, pow_ref,
               wcr_ref, wci_ref, dskip_ref, wglu_ref, bglu_ref, o_ref,
               x_ref, xb_ref, yp_ref, end_ref, seg_ref, carry_ref):
    lc, d = h_ref.shape
    ns = x_ref.shape[1] // 2
    seg = lc // SUBLANES
    lg = S5_LANE_GROUP

    @pl.when(pl.program_id(1) == 0)
    def _():
        carry_ref[...] = jnp.zeros_like(carry_ref)

    u = _prenorm(h_ref[...], ng_ref, mod_ref, 0, 0)
    u_perm = jnp.dot(perm_ref[...], u.astype(BF16),
                     preferred_element_type=F32).astype(BF16)

    n_blocks = wbu_ref.shape[0]
    blocks_per_part = n_blocks // 2
    for cb in range(n_blocks):
        slab = (cb % blocks_per_part) // 2
        x_ref[:, cb * 256:(cb + 1) * 256] = jnp.dot(
            u_perm[:, slab * LANES:(slab + 1) * LANES], wbu_ref[cb],
            preferred_element_type=F32)

    for g in range(ns // lg):
        re = slice(g * lg, (g + 1) * lg)
        im = slice(ns + g * lg, ns + (g + 1) * lg)
        ar = jnp.broadcast_to(pow_ref[0:1, re], (SUBLANES, lg))
        ai = jnp.broadcast_to(pow_ref[0:1, im], (SUBLANES, lg))

        def end_step(k, carry, re=re, im=im, ar=ar, ai=ai):
            xr, xi = carry
            rows = pl.ds(pl.multiple_of(k * SUBLANES, SUBLANES), SUBLANES)
            return (ar * xr - ai * xi + x_ref[rows, re],
                    ar * xi + ai * xr + x_ref[rows, im])

        zero = jnp.zeros((SUBLANES, lg), F32)
        end_r, end_i = lax.fori_loop(0, seg, end_step, (zero, zero))
        end_ref[:, re] = end_r
        end_ref[:, im] = end_i

    apr, api = pow_ref[seg - 1:seg, 0:ns], pow_ref[seg - 1:seg, ns:2 * ns]
    sr, si = carry_ref[0:1, 0:ns], carry_ref[0:1, ns:2 * ns]
    for j in range(SUBLANES):
        seg_ref[j:j + 1, 0:ns] = sr
        seg_ref[j:j + 1, ns:2 * ns] = si
        er = end_ref[j:j + 1, 0:ns]
        ei = end_ref[j:j + 1, ns:2 * ns]
        sr, si = apr * sr - api * si + er, apr * si + api * sr + ei
    carry_ref[0:1, 0:ns] = sr
    carry_ref[0:1, ns:2 * ns] = si

    for g in range(ns // lg):
        re = slice(g * lg, (g + 1) * lg)
        im = slice(ns + g * lg, ns + (g + 1) * lg)
        ar = jnp.broadcast_to(pow_ref[0:1, re], (SUBLANES, lg))
        ai = jnp.broadcast_to(pow_ref[0:1, im], (SUBLANES, lg))

        def scan_step(m, carry, re=re, im=im, ar=ar, ai=ai):
            xr, xi = carry
            out_r, out_i = [], []
            for half in range(2):
                k = 2 * m + half
                rows = pl.ds(pl.multiple_of(k * SUBLANES, SUBLANES), SUBLANES)
                xr, xi = (ar * xr - ai * xi + x_ref[rows, re],
                          ar * xi + ai * xr + x_ref[rows, im])
                out_r.append(xr)
                out_i.append(xi)
            rows2 = pl.ds(pl.multiple_of(m * 2 * SUBLANES, 2 * SUBLANES), 2 * SUBLANES)
            xb_ref[rows2, re] = jnp.concatenate(out_r, axis=0).astype(BF16)
            xb_ref[rows2, im] = jnp.concatenate(out_i, axis=0).astype(BF16)
            return xr, xi

        lax.fori_loop(0, seg // 2, scan_step, (seg_ref[:, re], seg_ref[:, im]))

    n_out = wcr_ref.shape[0]
    kw = ns // n_out
    for ob in range(n_out):
        yp_ref[:, ob * LANES:(ob + 1) * LANES] = (
            jnp.dot(xb_ref[:, ob * kw:(ob + 1) * kw], wcr_ref[ob],
                    preferred_element_type=F32)
            + jnp.dot(xb_ref[:, ns + ob * kw:ns + (ob + 1) * kw], wci_ref[ob],
                      preferred_element_type=F32))

    hi, mid, lo = _split3(yp_ref[...])
    pt = permt_ref[...]
    y = (jnp.dot(pt, hi, preferred_element_type=F32)
         + jnp.dot(pt, mid, preferred_element_type=F32)
         + jnp.dot(pt, lo, preferred_element_type=F32))
    y = y + dskip_ref[...] * u
    y = 0.5 * y * (1.0 + jnp.tanh(0.7978845608028654 * (y + 0.044715 * (y * y * y))))
    z = jnp.dot(y.astype(BF16), wglu_ref[...],
                preferred_element_type=F32) + bglu_ref[...]
    o_ref[...] = z[:, :d] * _sigmoid(z[:, d:])


def _s5_mixer(h, mod_l, ng_l, lam_re, lam_im, log_dt, b_re, b_im, c_re, c_im,
              d_skip, w_glu, b_glu):
    bsz, seq, d = h.shape
    g, p = lam_re.shape
    gc = b_re.shape[2]
    ns = g * p
    lc, seg = S5_CHUNK, S5_SEG
    assert gc == S5_GROUP and p == S5_STATE and g * gc == d

    flat = lambda a: a.reshape(1, ns)
    chan_major = lambda a: a.transpose(2, 0, 1).reshape(gc, ns)
    bbr, bbi, powers = pl.pallas_call(
        _s5_param_kernel,
        out_shape=(jax.ShapeDtypeStruct((gc, ns), F32),
                   jax.ShapeDtypeStruct((gc, ns), F32),
                   jax.ShapeDtypeStruct((seg, 2 * ns), F32)),
        compiler_params=pltpu.CompilerParams(vmem_limit_bytes=VMEM_LIMIT),
        name="s5_params",
    )(flat(lam_re), flat(lam_im), flat(jnp.repeat(log_dt, p)),
      chan_major(b_re), chan_major(b_im))

    def bu_blocks(bb):
        t = bb.reshape(gc, ns // 256, 4, p)
        t = jnp.einsum('cbgp,gh->bhcgp', t, jnp.eye(4, dtype=F32))
        t = t.reshape(ns // 256, 4 * gc, 256)
        half = jnp.eye(2, dtype=F32)[jnp.arange(ns // 256) % 2]
        return jnp.einsum('bkn,bq->bqkn', t, half).reshape(ns // 256, LANES, 256)

    wbu = jnp.concatenate([bu_blocks(bbr), bu_blocks(bbi)], axis=0).astype(BF16)

    def c_blocks(cc):
        t = cc.reshape(d // LANES, 8, gc, p)
        t = jnp.einsum('ogcp,gh->ogphc', t, jnp.eye(8, dtype=F32))
        return t.reshape(d // LANES, 8 * p, LANES).astype(BF16)

    wcr, wci = c_blocks(c_re), c_blocks(-c_im)

    r = np.arange(lc)
    perm_np = np.zeros((lc, lc), np.float32)
    perm_np[r, (r % SUBLANES) * seg + r // SUBLANES] = 1.0
    perm = jnp.asarray(perm_np, BF16)
    perm_t = jnp.asarray(perm_np.T, BF16)

    kw = ns // (d // LANES)
    return pl.pallas_call(
        _s5_kernel,
        grid=(bsz, seq // lc),
        in_specs=[
            pl.BlockSpec((None, lc, d), lambda b, i: (b, i, 0)),
            pl.BlockSpec((None, 6, d), lambda b, i: (b, 0, 0)),
            _const_spec((4, d)),
            _const_spec((lc, lc)),
            _const_spec((lc, lc)),
            _const_spec((2 * ns // 256, LANES, 256)),
            _const_spec((seg, 2 * ns)),
            _const_spec((d // LANES, kw, LANES)),
            _const_spec((d // LANES, kw, LANES)),
            _const_spec((1, d)),
            _const_spec((d, 2 * d)),
            _const_spec((1, 2 * d)),
        ],
        out_specs=pl.BlockSpec((None, lc, d), lambda b, i: (b, i, 0)),
        out_shape=jax.ShapeDtypeStruct((bsz, seq, d), F32),
        scratch_shapes=[pltpu.VMEM((lc, 2 * ns), F32),
                        pltpu.VMEM((lc, 2 * ns), BF16),
                        pltpu.VMEM((lc, d), F32),
                        pltpu.VMEM((SUBLANES, 2 * ns), F32),
                        pltpu.VMEM((SUBLANES, 2 * ns), F32),
                        pltpu.VMEM((1, 2 * ns), F32)],
        compiler_params=_params("arbitrary", "arbitrary"),
        name="s5_mixer",
    )(h, mod_l, ng_l, perm, perm_t, wbu, powers, wcr, wci,
      d_skip.reshape(1, d), w_glu.astype(BF16), b_glu.reshape(1, 2 * d))


def _conv_kernel(h_ref, mod_ref, ng_ref, w1_ref, b1_ref, wdw_ref, bdw_ref,
                 lng_ref, lnb_ref, w2_ref, b2_ref, o_ref, buf_ref, sh_ref, acc_ref):
    tm, d = h_ref.shape
    halo = CONV_HALO

    @pl.when(pl.program_id(1) == 0)
    def _():
        buf_ref[0:halo, :] = jnp.zeros((halo, d), F32)

    u = _prenorm(h_ref[...], ng_ref, mod_ref, 0, 0)
    z = jnp.dot(u.astype(BF16), w1_ref[...],
                preferred_element_type=F32) + b1_ref[...]
    buf_ref[halo:halo + tm, :] = z[:, :d] * _sigmoid(z[:, d:])

    width = wdw_ref.shape[0] // SUBLANES
    first = halo - (width - 1)
    span = sh_ref.shape[1]
    for s in range(1, SUBLANES):
        sh_ref[s - 1] = buf_ref[s:s + span, :]

    rg = CONV_ROW_GROUP

    def tap_rows(r, carry):
        r0 = pl.multiple_of(r * rg, rg)
        slabs = rg // SUBLANES
        parts = [jnp.zeros((SUBLANES, d), F32) + bdw_ref[...] for _ in range(slabs)]
        for j in range(width):
            a, s = divmod(first + j, SUBLANES)
            tap = wdw_ref[j * SUBLANES:(j + 1) * SUBLANES, :]
            for i in range(slabs):
                rows = pl.ds(r0 + (a + i) * SUBLANES, SUBLANES)
                src = buf_ref[rows, :] if s == 0 else sh_ref[s - 1, rows, :]
                parts[i] = parts[i] + tap * src
        for i in range(slabs):
            acc_ref[pl.ds(r0 + i * SUBLANES, SUBLANES), :] = parts[i]
        return carry

    lax.fori_loop(0, tm // rg, tap_rows, 0)
    buf_ref[0:halo, :] = buf_ref[tm:tm + halo, :]
    acc = acc_ref[...]

    mu = jnp.mean(acc, axis=-1, keepdims=True)
    cen = acc - mu
    var = jnp.mean(cen * cen, axis=-1, keepdims=True)
    y = cen * lax.rsqrt(var + EPS) * lng_ref[...] + lnb_ref[...]
    y = y * _sigmoid(y)
    o_ref[...] = jnp.dot(y.astype(BF16), w2_ref[...],
                         preferred_element_type=F32) + b2_ref[...]


def _conv_mixer(h, mod_l, ng_l, w_pw1, b_pw1, w_dw, b_dw, ln_g, ln_b, w_pw2, b_pw2):
    bsz, seq, d = h.shape
    tm = TOK_TILE
    width = w_dw.shape[0]
    assert width - 1 <= CONV_HALO <= tm
    row = lambda a: a.reshape(1, -1)
    return pl.pallas_call(
        _conv_kernel,
        grid=(bsz, seq // tm),
        in_specs=[
            pl.BlockSpec((None, tm, d), lambda b, i: (b, i, 0)),
            pl.BlockSpec((None, 6, d), lambda b, i: (b, 0, 0)),
            _const_spec((4, d)),
            _const_spec((d, 2 * d)),
            _const_spec((1, 2 * d)),
            _const_spec((width * SUBLANES, d)),
            _const_spec((1, d)),
            _const_spec((1, d)),
            _const_spec((1, d)),
            _const_spec((d, d)),
            _const_spec((1, d)),
        ],
        out_specs=pl.BlockSpec((None, tm, d), lambda b, i: (b, i, 0)),
        out_shape=jax.ShapeDtypeStruct((bsz, seq, d), F32),
        scratch_shapes=[pltpu.VMEM((CONV_HALO + tm, d), F32),
                        pltpu.VMEM((SUBLANES - 1, CONV_HALO + tm - SUBLANES, d), F32),
                        pltpu.VMEM((tm, d), F32)],
        compiler_params=_params("arbitrary", "arbitrary"),
        name="conformer_conv",
    )(h, mod_l, ng_l, w_pw1.astype(BF16), row(b_pw1), jnp.repeat(w_dw, SUBLANES, axis=0), row(b_dw),
      row(ln_g), row(ln_b), w_pw2.astype(BF16), row(b_pw2))


def kernel(x, c, norm_g, w_mod, b_mod, sb_w_qkv, sb_w_o, s5_lam_re, s5_lam_im, s5_log_dt, s5_b_re, s5_b_im, s5_c_re, s5_c_im, s5_d, s5_w_glu, s5_b_glu, cv_w_pw1, cv_b_pw1, cv_w_dw, cv_b_dw, cv_ln_g, cv_ln_b, cv_w_pw2, cv_b_pw2, ffn_w_gate, ffn_w_up, ffn_w_down):
    depth = norm_g.shape[0]
    d = x.shape[-1]
    mod_all = _modulation(c, w_mod, b_mod)
    h = x
    for layer in range(depth):
        kind, j = layer % N_MIXERS, layer // N_MIXERS
        mod_l, ng_l = mod_all[layer], norm_g[layer]
        w_o = None
        if kind == 0:
            qkv = _sb_qkv(h, mod_l, ng_l, sb_w_qkv[j].astype(BF16))
            m = _sb_attention(qkv, d)
            w_o = sb_w_o[j].astype(BF16)
        elif kind == 1:
            m = _s5_mixer(h, mod_l, ng_l, s5_lam_re[j], s5_lam_im[j], s5_log_dt[j],
                          s5_b_re[j], s5_b_im[j], s5_c_re[j], s5_c_im[j], s5_d[j],
                          s5_w_glu[j], s5_b_glu[j])
        else:
            m = _conv_mixer(h, mod_l, ng_l, cv_w_pw1[j], cv_b_pw1[j], cv_w_dw[j],
                            cv_b_dw[j], cv_ln_g[j], cv_ln_b[j], cv_w_pw2[j], cv_b_pw2[j])
        h = _resid_ffn(h, m, mod_l, ng_l, w_o, ffn_w_gate[layer].astype(BF16),
                       ffn_w_up[layer].astype(BF16), ffn_w_down[layer].astype(BF16))
    return h
```

```python
import functools

import numpy as np
import jax
import jax.numpy as jnp
from jax import lax
from jax.experimental import pallas as pl
from jax.experimental.pallas import tpu as pltpu

F32 = jnp.float32
BF16 = jnp.bfloat16

EPS = 1e-6
N_MIXERS = 3
SB_HEADS = 16
S5_GROUP = 16
S5_STATE = 64
CONV_WIDTH = 31

LANES = 128
SUBLANES = 8
VMEM_LIMIT = 56 * 1024 * 1024

TOK_TILE = 512
ATT_Q = 128
ATT_PAST = 256
ATT_PAD = 2 * ATT_PAST
ATT_HEAD_PAIRS = 4
ATT_CUTOFF = 106.0
S5_SEG = 64
S5_CHUNK = SUBLANES * S5_SEG
S5_LANE_GROUP = 1024
CONV_HALO = 32
CONV_ROW_GROUP = 32


def _params(*sem):
    return pltpu.CompilerParams(dimension_semantics=sem, vmem_limit_bytes=VMEM_LIMIT)


def _const_spec(shape):
    nd = len(shape)
    return pl.BlockSpec(shape, lambda *_: (0,) * nd, pipeline_mode=pl.Buffered(1))


def _layer_spec(stacked, layer):
    _, rows, cols = stacked.shape
    return pl.BlockSpec((None, rows, cols), lambda *_: (layer, 0, 0),
                        pipeline_mode=pl.Buffered(1))


def _rms(x, g):
    ms = jnp.mean(x * x, axis=-1, keepdims=True)
    return x * lax.rsqrt(ms + EPS) * g


def _sigmoid(x):
    return 1.0 / (1.0 + jnp.exp(-x))


def _prenorm(x, ng_ref, mod_ref, norm_row, shift_row):
    y = _rms(x, ng_ref[norm_row:norm_row + 1, :])
    return (y * (1.0 + mod_ref[shift_row + 1:shift_row + 2, :])
            + mod_ref[shift_row:shift_row + 1, :])


def _mod_kernel(c_ref, w_ref, b_ref, o_ref):
    c = c_ref[...]
    o_ref[...] = jnp.dot(c * _sigmoid(c), w_ref[...],
                         preferred_element_type=F32) + b_ref[...]


def _modulation(c, w_mod, b_mod):
    depth, d, e = w_mod.shape
    bsz = c.shape[0]
    rows = SUBLANES
    c_pad = jnp.zeros((rows, d), F32).at[:bsz].set(c)
    tn = d
    out = pl.pallas_call(
        _mod_kernel,
        grid=(depth, e // tn),
        in_specs=[
            pl.BlockSpec((rows, d), lambda l, j: (0, 0)),
            pl.BlockSpec((None, d, tn), lambda l, j: (l, 0, j)),
            pl.BlockSpec((None, 1, tn), lambda l, j: (l, 0, j)),
        ],
        out_specs=pl.BlockSpec((None, rows, tn), lambda l, j: (l, 0, j)),
        out_shape=jax.ShapeDtypeStruct((depth, rows, e), F32),
        compiler_params=_params("arbitrary", "arbitrary"),
        name="adaln_mod",
    )(c_pad, w_mod, b_mod.reshape(depth, 1, e))
    return out[:, :bsz].reshape(depth, bsz, 6, d)


def _sb_qkv_kernel(h_ref, mod_ref, ng_ref, w_ref, o_ref):
    @pl.when(pl.program_id(1) == 0)
    def _():
        o_ref[...] = jnp.zeros_like(o_ref)

    @pl.when(pl.program_id(1) > 0)
    def _():
        u = _prenorm(h_ref[...], ng_ref, mod_ref, 0, 0)
        o_ref[...] = jnp.dot(u.astype(BF16), w_ref[...],
                             preferred_element_type=F32).astype(BF16)


def _sb_qkv(h, mod_l, ng_l, w_qkv, layer):
    bsz, seq, d = h.shape
    n = w_qkv.shape[2]
    tm = ATT_PAD
    return pl.pallas_call(
        _sb_qkv_kernel,
        grid=(bsz, seq // tm + 1),
        in_specs=[
            pl.BlockSpec((None, tm, d), lambda b, i: (b, jnp.maximum(i - 1, 0), 0)),
            pl.BlockSpec((None, 6, d), lambda b, i: (b, 0, 0)),
            _const_spec((4, d)),
            _layer_spec(w_qkv, layer),
        ],
        out_specs=pl.BlockSpec((None, tm, n), lambda b, i: (b, i, 0)),
        out_shape=jax.ShapeDtypeStruct((bsz, seq + tm, n), BF16),
        compiler_params=_params("arbitrary", "arbitrary"),
        name="sb_qkv",
    )(h, mod_l, ng_l, w_qkv)


def _split2(x):
    hi = x.astype(BF16)
    return hi, (x - hi.astype(F32)).astype(BF16)


def _suffix_sums(x, st_ref):
    hi, lo = _split2(x)
    return jnp.dot(jnp.concatenate([hi, lo], axis=1), st_ref[...],
                   preferred_element_type=F32)


def _sb_attn_kernel(q_ref, k_ref, v_ref, st_past_ref, st_diag_ref, o_ref,
                    acc_ref, car_ref, *, scale):
    qb, past = ATT_Q, ATT_PAST
    hd = LANES // 2
    pairs = q_ref.shape[1] // LANES
    qi = pl.program_id(2)
    q0 = pl.multiple_of(qi * qb, qb)
    q = q_ref[...].astype(F32) * scale
    lane = lax.broadcasted_iota(jnp.int32, (qb, LANES), 1)
    q_pairs = []
    for p in range(pairs):
        qp = q[:, p * LANES:(p + 1) * LANES]
        q_pairs.append(jnp.concatenate(
            [jnp.where(lane < hd, qp, 0.0), jnp.where(lane >= hd, qp, 0.0)],
            axis=0).astype(BF16))
    rows = 2 * pairs * qb
    row = lax.broadcasted_iota(jnp.int32, (rows, qb), 0) & (qb - 1)
    col = lax.broadcasted_iota(jnp.int32, (rows, qb), 1)
    causal = col < row

    def softplus_terms(z):
        sp = jnp.log(1.0 + jnp.exp(-jnp.abs(z)))
        return jnp.minimum(z, 0.0) - sp, jnp.maximum(z, 0.0) + sp

    def scores(k_win):
        return jnp.concatenate(
            [lax.dot_general(q_pairs[p], k_win[:, p * LANES:(p + 1) * LANES],
                             (((1,), (1,)), ((), ())), preferred_element_type=F32)
             for p in range(pairs)], axis=0)

    def values(w, v_win):
        wb = w.astype(BF16)
        return jnp.concatenate(
            [jnp.dot(wb[p * 2 * qb:(p + 1) * 2 * qb], v_win[:, p * LANES:(p + 1) * LANES],
                     preferred_element_type=F32) for p in range(pairs)], axis=0)

    start = pl.multiple_of(q0 + (ATT_PAD - past), qb)
    log_beta, neg_keep = softplus_terms(scores(k_ref[pl.ds(start, past + qb), :]))
    sums_d = _suffix_sums(jnp.where(causal, neg_keep[:, past:], 0.0), st_diag_ref)
    later_d, total_d = sums_d[:, :qb], sums_d[:, qb:]
    sums_p = _suffix_sums(neg_keep[:, :past], st_past_ref)
    later_p = sums_p[:, :past] + jnp.concatenate([total_d] * (past // LANES), axis=1)
    w = jnp.concatenate(
        [jnp.exp(log_beta[:, :past] - later_p),
         jnp.where(causal, jnp.exp(log_beta[:, past:] - later_d), 0.0)], axis=1)
    acc_ref[...] = values(w, v_ref[pl.ds(start, past + qb), :])
    car = total_d + sums_p[:, past:]
    car_ref[...] = car

    def cond(carry):
        ws, lowest = carry
        return jnp.logical_and(ws > ATT_PAD - past, lowest < ATT_CUTOFF)

    def body(carry):
        ws, _ = carry
        ws = pl.multiple_of(ws, qb)
        log_beta, neg_keep = softplus_terms(scores(k_ref[pl.ds(ws, past), :]))
        sums = _suffix_sums(neg_keep, st_past_ref)
        car = car_ref[...]
        later = sums[:, :past] + jnp.concatenate([car] * (past // LANES), axis=1)
        acc_ref[...] += values(jnp.exp(log_beta - later), v_ref[pl.ds(ws, past), :])
        car = car + sums[:, past:]
        car_ref[...] = car
        return ws - past, jnp.min(car)

    lax.while_loop(cond, body, (q0 + (ATT_PAD - 2 * past), jnp.min(car)))
    for p in range(pairs):
        o_ref[:, p * LANES:(p + 1) * LANES] = jnp.where(
            lane < hd, acc_ref[2 * p * qb:(2 * p + 1) * qb, :],
            acc_ref[(2 * p + 1) * qb:(2 * p + 2) * qb, :]).astype(BF16)


def _suffix_matrix(width):
    j = np.arange(width)[:, None]
    s = np.arange(width + LANES)[None, :]
    st = ((s >= width) | (j > s)).astype(np.float32)
    return jnp.asarray(np.concatenate([st, st], axis=0), BF16)


def _sb_attention(qkv, d):
    bsz, padded, _ = qkv.shape
    seq = padded - ATT_PAD
    qb, past = ATT_Q, ATT_PAST
    head_dim = d // SB_HEADS
    assert 2 * head_dim == LANES and ATT_PAD == 2 * past and past % qb == 0
    width = ATT_HEAD_PAIRS * LANES
    groups = d // width
    kern = functools.partial(_sb_attn_kernel, scale=head_dim ** -0.5)
    return pl.pallas_call(
        kern,
        grid=(bsz, groups, seq // qb),
        in_specs=[
            pl.BlockSpec((None, qb, width), lambda b, p, i: (b, i + ATT_PAD // qb, p)),
            pl.BlockSpec((None, padded, width), lambda b, p, i: (b, 0, groups + p)),
            pl.BlockSpec((None, padded, width), lambda b, p, i: (b, 0, 2 * groups + p)),
            _const_spec((2 * past, past + LANES)),
            _const_spec((2 * qb, qb + LANES)),
        ],
        out_specs=pl.BlockSpec((None, qb, width), lambda b, p, i: (b, i, p)),
        out_shape=jax.ShapeDtypeStruct((bsz, seq, d), BF16),
        scratch_shapes=[pltpu.VMEM((2 * ATT_HEAD_PAIRS * qb, LANES), F32),
                        pltpu.VMEM((2 * ATT_HEAD_PAIRS * qb, LANES), F32)],
        compiler_params=_params("arbitrary", "arbitrary", "arbitrary"),
        name="sb_attention",
    )(qkv, qkv, qkv, _suffix_matrix(past), _suffix_matrix(qb))


def _resid_ffn_kernel(*refs, has_wo, ff_chunk):
    if has_wo:
        h_ref, m_ref, mod_ref, ng_ref, wo_ref, wg_ref, wu_ref, wd_ref, o_ref = refs
        m = jnp.dot(m_ref[...], wo_ref[...], preferred_element_type=F32)
    else:
        h_ref, m_ref, mod_ref, ng_ref, wg_ref, wu_ref, wd_ref, o_ref = refs
        m = m_ref[...]
    h1 = h_ref[...] + mod_ref[2:3, :] * _rms(m, ng_ref[1:2, :])
    u = _prenorm(h1, ng_ref, mod_ref, 2, 3).astype(BF16)
    d_ff = wg_ref.shape[1]
    acc = jnp.zeros_like(h1)
    for c in range(d_ff // ff_chunk):
        lo, hi = c * ff_chunk, (c + 1) * ff_chunk
        g = jnp.dot(u, wg_ref[:, lo:hi], preferred_element_type=F32)
        up = jnp.dot(u, wu_ref[:, lo:hi], preferred_element_type=F32)
        a = (g * _sigmoid(g) * up).astype(BF16)
        acc = acc + jnp.dot(a, wd_ref[lo:hi, :], preferred_element_type=F32)
    o_ref[...] = h1 + mod_ref[5:6, :] * _rms(acc, ng_ref[3:4, :])


def _resid_ffn(h, m, mod_l, ng_l, w_o, o_layer, w_gate, w_up, w_down, layer):
    bsz, seq, d = h.shape
    d_ff = w_gate.shape[2]
    tm = TOK_TILE
    ff_chunk = d_ff // 11
    assert ff_chunk % LANES == 0
    has_wo = w_o is not None
    tile = lambda: pl.BlockSpec((None, tm, d), lambda b, i: (b, i, 0))
    in_specs = [tile(), tile(),
                pl.BlockSpec((None, 6, d), lambda b, i: (b, 0, 0)),
                _const_spec((4, d))]
    args = [h, m, mod_l, ng_l]
    if has_wo:
        in_specs.append(_layer_spec(w_o, o_layer))
        args.append(w_o)
    in_specs += [_layer_spec(w, layer) for w in (w_gate, w_up, w_down)]
    args += [w_gate, w_up, w_down]
    return pl.pallas_call(
        functools.partial(_resid_ffn_kernel, has_wo=has_wo, ff_chunk=ff_chunk),
        grid=(bsz, seq // tm),
        in_specs=in_specs,
        out_specs=tile(),
        out_shape=jax.ShapeDtypeStruct((bsz, seq, d), F32),
        compiler_params=_params("arbitrary", "arbitrary"),
        name="resid_ffn_wo" if has_wo else "resid_ffn",
    )(*args)


def _s5_param_kernel(lr_ref, li_ref, ldt_ref, br_ref, bi_ref,
                     bbr_ref, bbi_ref, pow_ref):
    n = lr_ref.shape[1]
    lr, li = lr_ref[...], li_ref[...]
    dt = jnp.exp(ldt_ref[...])
    mag = jnp.exp(lr * dt)
    ar, ai = mag * jnp.cos(li * dt), mag * jnp.sin(li * dt)
    den = lr * lr + li * li
    er = ((ar - 1.0) * lr + ai * li) / den
    ei = (ai * lr - (ar - 1.0) * li) / den
    br, bi = br_ref[...], bi_ref[...]
    bbr_ref[...] = er * br - ei * bi
    bbi_ref[...] = er * bi + ei * br
    pr, pi = ar, ai
    for k in range(pow_ref.shape[0]):
        pow_ref[k:k + 1, 0:n] = pr
        pow_ref[k:k + 1, n:2 * n] = pi
        pr, pi = pr * ar - pi * ai, pr * ai + pi * ar


def _s5_kernel(h_ref, mod_ref, ng_ref, perm_ref, permt_ref, wbu_ref, pow_ref,
               wcr_ref, wci_ref, dskip_ref, wglu_ref, bglu_ref, o_ref,
               x_ref, xb_ref, yp_ref, end_ref, seg_ref, carry_ref):
    lc, d = h_ref.shape
    ns = x_ref.shape[1] // 2
    seg = lc // SUBLANES
    lg = S5_LANE_GROUP

    @pl.when(pl.program_id(1) == 0)
    def _():
        carry_ref[...] = jnp.zeros_like(carry_ref)

    u = _prenorm(h_ref[...], ng_ref, mod_ref, 0, 0)
    u_perm = jnp.dot(perm_ref[...], u.astype(BF16),
                     preferred_element_type=F32).astype(BF16)

    n_blocks = wbu_ref.shape[0]
    blocks_per_part = n_blocks // 2
    for cb in range(n_blocks):
        slab = (cb % blocks_per_part) // 2
        x_ref[:, cb * 256:(cb + 1) * 256] = jnp.dot(
            u_perm[:, slab * LANES:(slab + 1) * LANES], wbu_ref[cb],
            preferred_element_type=F32)

    for g in range(ns // lg):
        re = slice(g * lg, (g + 1) * lg)
        im = slice(ns + g * lg, ns + (g + 1) * lg)
        ar = jnp.broadcast_to(pow_ref[0:1, re], (SUBLANES, lg))
        ai = jnp.broadcast_to(pow_ref[0:1, im], (SUBLANES, lg))

        def end_step(k, carry, re=re, im=im, ar=ar, ai=ai):
            xr, xi = carry
            rows = pl.ds(pl.multiple_of(k * SUBLANES, SUBLANES), SUBLANES)
            return (ar * xr - ai * xi + x_ref[rows, re],
                    ar * xi + ai * xr + x_ref[rows, im])

        zero = jnp.zeros((SUBLANES, lg), F32)
        end_r, end_i = lax.fori_loop(0, seg, end_step, (zero, zero))
        end_ref[:, re] = end_r
        end_ref[:, im] = end_i

    apr, api = pow_ref[seg - 1:seg, 0:ns], pow_ref[seg - 1:seg, ns:2 * ns]
    sr, si = carry_ref[0:1, 0:ns], carry_ref[0:1, ns:2 * ns]
    for j in range(SUBLANES):
        seg_ref[j:j + 1, 0:ns] = sr
        seg_ref[j:j + 1, ns:2 * ns] = si
        er = end_ref[j:j + 1, 0:ns]
        ei = end_ref[j:j + 1, ns:2 * ns]
        sr, si = apr * sr - api * si + er, apr * si + api * sr + ei
    carry_ref[0:1, 0:ns] = sr
    carry_ref[0:1, ns:2 * ns] = si

    for g in range(ns // lg):
        re = slice(g * lg, (g + 1) * lg)
        im = slice(ns + g * lg, ns + (g + 1) * lg)
        ar = jnp.broadcast_to(pow_ref[0:1, re], (SUBLANES, lg))
        ai = jnp.broadcast_to(pow_ref[0:1, im], (SUBLANES, lg))

        def scan_step(m, carry, re=re, im=im, ar=ar, ai=ai):
            xr, xi = carry
            out_r, out_i = [], []
            for half in range(2):
                k = 2 * m + half
                rows = pl.ds(pl.multiple_of(k * SUBLANES, SUBLANES), SUBLANES)
                xr, xi = (ar * xr - ai * xi + x_ref[rows, re],
                          ar * xi + ai * xr + x_ref[rows, im])
                out_r.append(xr)
                out_i.append(xi)
            rows2 = pl.ds(pl.multiple_of(m * 2 * SUBLANES, 2 * SUBLANES), 2 * SUBLANES)
            xb_ref[rows2, re] = jnp.concatenate(out_r, axis=0).astype(BF16)
            xb_ref[rows2, im] = jnp.concatenate(out_i, axis=0).astype(BF16)
            return xr, xi

        lax.fori_loop(0, seg // 2, scan_step, (seg_ref[:, re], seg_ref[:, im]))

    n_out = wcr_ref.shape[0]
    kw = ns // n_out
    for ob in range(n_out):
        yp_ref[:, ob * LANES:(ob + 1) * LANES] = (
            jnp.dot(xb_ref[:, ob * kw:(ob + 1) * kw], wcr_ref[ob],
                    preferred_element_type=F32)
            + jnp.dot(xb_ref[:, ns + ob * kw:ns + (ob + 1) * kw], wci_ref[ob],
                      preferred_element_type=F32))

    hi, lo = _split2(yp_ref[...])
    pt = permt_ref[...]
    y = (jnp.dot(pt, hi, preferred_element_type=F32)
         + jnp.dot(pt, lo, preferred_element_type=F32))
    y = y + dskip_ref[...] * u
    y = 0.5 * y * (1.0 + jnp.tanh(0.7978845608028654 * (y + 0.044715 * (y * y * y))))
    z = jnp.dot(y.astype(BF16), wglu_ref[...],
                preferred_element_type=F32) + bglu_ref[...]
    o_ref[...] = z[:, :d] * _sigmoid(z[:, d:])


def _s5_mixer(h, mod_l, ng_l, lam_re, lam_im, log_dt, b_re, b_im, c_re, c_im,
              d_skip, w_glu, b_glu):
    bsz, seq, d = h.shape
    g, p = lam_re.shape
    gc = b_re.shape[2]
    ns = g * p
    lc, seg = S5_CHUNK, S5_SEG
    assert gc == S5_GROUP and p == S5_STATE and g * gc == d

    flat = lambda a: a.reshape(1, ns)
    chan_major = lambda a: a.transpose(2, 0, 1).reshape(gc, ns)
    bbr, bbi, powers = pl.pallas_call(
        _s5_param_kernel,
        out_shape=(jax.ShapeDtypeStruct((gc, ns), F32),
                   jax.ShapeDtypeStruct((gc, ns), F32),
                   jax.ShapeDtypeStruct((seg, 2 * ns), F32)),
        compiler_params=pltpu.CompilerParams(vmem_limit_bytes=VMEM_LIMIT),
        name="s5_params",
    )(flat(lam_re), flat(lam_im), flat(jnp.repeat(log_dt, p)),
      chan_major(b_re), chan_major(b_im))

    def bu_blocks(bb):
        t = bb.reshape(gc, ns // 256, 4, p)
        t = jnp.einsum('cbgp,gh->bhcgp', t, jnp.eye(4, dtype=F32))
        t = t.reshape(ns // 256, 4 * gc, 256)
        half = jnp.eye(2, dtype=F32)[jnp.arange(ns // 256) % 2]
        return jnp.einsum('bkn,bq->bqkn', t, half).reshape(ns // 256, LANES, 256)

    wbu = jnp.concatenate([bu_blocks(bbr), bu_blocks(bbi)], axis=0).astype(BF16)

    def c_blocks(cc):
        t = cc.reshape(d // LANES, 8, gc, p)
        t = jnp.einsum('ogcp,gh->ogphc', t, jnp.eye(8, dtype=F32))
        return t.reshape(d // LANES, 8 * p, LANES).astype(BF16)

    wcr, wci = c_blocks(c_re), c_blocks(-c_im)

    r = np.arange(lc)
    perm_np = np.zeros((lc, lc), np.float32)
    perm_np[r, (r % SUBLANES) * seg + r // SUBLANES] = 1.0
    perm = jnp.asarray(perm_np, BF16)
    perm_t = jnp.asarray(perm_np.T, BF16)

    kw = ns // (d // LANES)
    return pl.pallas_call(
        _s5_kernel,
        grid=(bsz, seq // lc),
        in_specs=[
            pl.BlockSpec((None, lc, d), lambda b, i: (b, i, 0)),
            pl.BlockSpec((None, 6, d), lambda b, i: (b, 0, 0)),
            _const_spec((4, d)),
            _const_spec((lc, lc)),
            _const_spec((lc, lc)),
            _const_spec((2 * ns // 256, LANES, 256)),
            _const_spec((seg, 2 * ns)),
            _const_spec((d // LANES, kw, LANES)),
            _const_spec((d // LANES, kw, LANES)),
            _const_spec((1, d)),
            _const_spec((d, 2 * d)),
            _const_spec((1, 2 * d)),
        ],
        out_specs=pl.BlockSpec((None, lc, d), lambda b, i: (b, i, 0)),
        out_shape=jax.ShapeDtypeStruct((bsz, seq, d), F32),
        scratch_shapes=[pltpu.VMEM((lc, 2 * ns), F32),
                        pltpu.VMEM((lc, 2 * ns), BF16),
                        pltpu.VMEM((lc, d), F32),
                        pltpu.VMEM((SUBLANES, 2 * ns), F32),
                        pltpu.VMEM((SUBLANES, 2 * ns), F32),
                        pltpu.VMEM((1, 2 * ns), F32)],
        compiler_params=_params("arbitrary", "arbitrary"),
        name="s5_mixer",
    )(h, mod_l, ng_l, perm, perm_t, wbu, powers, wcr, wci,
      d_skip.reshape(1, d), w_glu.astype(BF16), b_glu.reshape(1, 2 * d))


def _conv_kernel(h_ref, mod_ref, ng_ref, w1_ref, b1_ref, wdw_ref, bdw_ref,
                 lng_ref, lnb_ref, w2_ref, b2_ref, o_ref, buf_ref, sh_ref, acc_ref):
    tm, d = h_ref.shape
    halo = CONV_HALO

    @pl.when(pl.program_id(1) == 0)
    def _():
        buf_ref[0:halo, :] = jnp.zeros((halo, d), F32)

    u = _prenorm(h_ref[...], ng_ref, mod_ref, 0, 0)
    z = jnp.dot(u.astype(BF16), w1_ref[...],
                preferred_element_type=F32) + b1_ref[...]
    buf_ref[halo:halo + tm, :] = z[:, :d] * _sigmoid(z[:, d:])

    width = wdw_ref.shape[0] // SUBLANES
    first = halo - (width - 1)
    span = sh_ref.shape[1]
    for s in range(1, SUBLANES):
        sh_ref[s - 1] = buf_ref[s:s + span, :]

    rg = CONV_ROW_GROUP

    def tap_rows(r, carry):
        r0 = pl.multiple_of(r * rg, rg)
        slabs = rg // SUBLANES
        parts = [jnp.zeros((SUBLANES, d), F32) + bdw_ref[...] for _ in range(slabs)]
        for j in range(width):
            a, s = divmod(first + j, SUBLANES)
            tap = wdw_ref[j * SUBLANES:(j + 1) * SUBLANES, :]
            for i in range(slabs):
                rows = pl.ds(r0 + (a + i) * SUBLANES, SUBLANES)
                src = buf_ref[rows, :] if s == 0 else sh_ref[s - 1, rows, :]
                parts[i] = parts[i] + tap * src
        for i in range(slabs):
            acc_ref[pl.ds(r0 + i * SUBLANES, SUBLANES), :] = parts[i]
        return carry

    lax.fori_loop(0, tm // rg, tap_rows, 0)
    buf_ref[0:halo, :] = buf_ref[tm:tm + halo, :]
    acc = acc_ref[...]

    mu = jnp.mean(acc, axis=-1, keepdims=True)
    cen = acc - mu
    var = jnp.mean(cen * cen, axis=-1, keepdims=True)
    y = cen * lax.rsqrt(var + EPS) * lng_ref[...] + lnb_ref[...]
    y = y * _sigmoid(y)
    o_ref[...] = jnp.dot(y.astype(BF16), w2_ref[...],
                         preferred_element_type=F32) + b2_ref[...]


def _conv_mixer(h, mod_l, ng_l, w_pw1, b_pw1, w_dw, b_dw, ln_g, ln_b, w_pw2, b_pw2):
    bsz, seq, d = h.shape
    tm = TOK_TILE
    width = w_dw.shape[0]
    assert width - 1 <= CONV_HALO <= tm
    row = lambda a: a.reshape(1, -1)
    return pl.pallas_call(
        _conv_kernel,
        grid=(bsz, seq // tm),
        in_specs=[
            pl.BlockSpec((None, tm, d), lambda b, i: (b, i, 0)),
            pl.BlockSpec((None, 6, d), lambda b, i: (b, 0, 0)),
            _const_spec((4, d)),
            _const_spec((d, 2 * d)),
            _const_spec((1, 2 * d)),
            _const_spec((width * SUBLANES, d)),
            _const_spec((1, d)),
            _const_spec((1, d)),
            _const_spec((1, d)),
            _const_spec((d, d)),
            _const_spec((1, d)),
        ],
        out_specs=pl.BlockSpec((None, tm, d), lambda b, i: (b, i, 0)),
        out_shape=jax.ShapeDtypeStruct((bsz, seq, d), F32),
        scratch_shapes=[pltpu.VMEM((CONV_HALO + tm, d), F32),
                        pltpu.VMEM((SUBLANES - 1, CONV_HALO + tm - SUBLANES, d), F32),
                        pltpu.VMEM((tm, d), F32)],
        compiler_params=_params("arbitrary", "arbitrary"),
        name="conformer_conv",
    )(h, mod_l, ng_l, w_pw1.astype(BF16), row(b_pw1), jnp.repeat(w_dw, SUBLANES, axis=0), row(b_dw),
      row(ln_g), row(ln_b), w_pw2.astype(BF16), row(b_pw2))


def kernel(x, c, norm_g, w_mod, b_mod, sb_w_qkv, sb_w_o, s5_lam_re, s5_lam_im, s5_log_dt, s5_b_re, s5_b_im, s5_c_re, s5_c_im, s5_d, s5_w_glu, s5_b_glu, cv_w_pw1, cv_b_pw1, cv_w_dw, cv_b_dw, cv_ln_g, cv_ln_b, cv_w_pw2, cv_b_pw2, ffn_w_gate, ffn_w_up, ffn_w_down):
    depth = norm_g.shape[0]
    d = x.shape[-1]
    mod_all = _modulation(c, w_mod, b_mod)
    qkv_bf, o_bf = sb_w_qkv.astype(BF16), sb_w_o.astype(BF16)
    gate_bf, up_bf, down_bf = (w.astype(BF16) for w in (ffn_w_gate, ffn_w_up, ffn_w_down))
    h = x
    for layer in range(depth):
        kind, j = layer % N_MIXERS, layer // N_MIXERS
        mod_l, ng_l = mod_all[layer], norm_g[layer]
        w_o = None
        if kind == 0:
            qkv = _sb_qkv(h, mod_l, ng_l, qkv_bf, j)
            m = _sb_attention(qkv, d)
            w_o = o_bf
        elif kind == 1:
            m = _s5_mixer(h, mod_l, ng_l, s5_lam_re[j], s5_lam_im[j], s5_log_dt[j],
                          s5_b_re[j], s5_b_im[j], s5_c_re[j], s5_c_im[j], s5_d[j],
                          s5_w_glu[j], s5_b_glu[j])
        else:
            m = _conv_mixer(h, mod_l, ng_l, cv_w_pw1[j], cv_b_pw1[j], cv_w_dw[j],
                            cv_b_dw[j], cv_ln_g[j], cv_ln_b[j], cv_w_pw2[j], cv_b_pw2[j])
        h = _resid_ffn(h, m, mod_l, ng_l, w_o, j, gate_bf, up_bf, down_bf, layer)
    return h
```

```python
import functools

import numpy as np
import jax
import jax.numpy as jnp
from jax import lax
from jax.experimental import pallas as pl
from jax.experimental.pallas import tpu as pltpu

F32 = jnp.float32
BF16 = jnp.bfloat16

EPS = 1e-6
LOG2E = 1.4426950408889634
N_MIXERS = 3
SB_HEADS = 16
S5_GROUP = 16
S5_STATE = 64
CONV_WIDTH = 31

LANES = 128
SUBLANES = 8
VMEM_LIMIT = 56 * 1024 * 1024

TOK_TILE = 512
ATT_Q = 128
ATT_PAST = 256
ATT_PAD = 2 * ATT_PAST
ATT_HEAD_PAIRS = 4
ATT_CUTOFF = 106.0
S5_SEG = 64
S5_CHUNK = SUBLANES * S5_SEG
S5_LANE_GROUP = 512
CONV_HALO = 32
CONV_ROW_GROUP = 32


def _params(*sem):
    return pltpu.CompilerParams(dimension_semantics=sem, vmem_limit_bytes=VMEM_LIMIT)


def _const_spec(shape):
    nd = len(shape)
    return pl.BlockSpec(shape, lambda *_: (0,) * nd, pipeline_mode=pl.Buffered(1))


def _layer_spec(stacked, layer):
    _, rows, cols = stacked.shape
    return pl.BlockSpec((None, rows, cols), lambda *_: (layer, 0, 0),
                        pipeline_mode=pl.Buffered(1))


def _rms(x, g):
    ms = jnp.mean(x * x, axis=-1, keepdims=True)
    return x * lax.rsqrt(ms + EPS) * g


def _sigmoid(x):
    return 1.0 / (1.0 + jnp.exp(-x))


def _prenorm(x, ng_ref, mod_ref, norm_row, shift_row):
    y = _rms(x, ng_ref[norm_row:norm_row + 1, :])
    return (y * (1.0 + mod_ref[shift_row + 1:shift_row + 2, :])
            + mod_ref[shift_row:shift_row + 1, :])


def _mod_kernel(c_ref, w_ref, b_ref, o_ref):
    c = c_ref[...]
    o_ref[...] = jnp.dot(c * _sigmoid(c), w_ref[...],
                         preferred_element_type=F32) + b_ref[...]


def _modulation(c, w_mod, b_mod):
    depth, d, e = w_mod.shape
    bsz = c.shape[0]
    rows = SUBLANES
    c_pad = jnp.zeros((rows, d), F32).at[:bsz].set(c)
    tn = d
    out = pl.pallas_call(
        _mod_kernel,
        grid=(depth, e // tn),
        in_specs=[
            pl.BlockSpec((rows, d), lambda l, j: (0, 0)),
            pl.BlockSpec((None, d, tn), lambda l, j: (l, 0, j)),
            pl.BlockSpec((None, 1, tn), lambda l, j: (l, 0, j)),
        ],
        out_specs=pl.BlockSpec((None, rows, tn), lambda l, j: (l, 0, j)),
        out_shape=jax.ShapeDtypeStruct((depth, rows, e), F32),
        compiler_params=_params("arbitrary", "arbitrary"),
        name="adaln_mod",
    )(c_pad, w_mod, b_mod.reshape(depth, 1, e))
    return out[:, :bsz].reshape(depth, bsz, 6, d)


def _sb_qkv_kernel(h_ref, mod_ref, ng_ref, w_ref, o_ref):
    @pl.when(pl.program_id(1) == 0)
    def _():
        o_ref[...] = jnp.zeros_like(o_ref)

    @pl.when(pl.program_id(1) > 0)
    def _():
        u = _prenorm(h_ref[...], ng_ref, mod_ref, 0, 0)
        o_ref[...] = jnp.dot(u.astype(BF16), w_ref[...],
                             preferred_element_type=F32).astype(BF16)


def _sb_qkv(h, mod_l, ng_l, w_qkv, layer):
    bsz, seq, d = h.shape
    n = w_qkv.shape[2]
    tm = ATT_PAD
    return pl.pallas_call(
        _sb_qkv_kernel,
        grid=(bsz, seq // tm + 1),
        in_specs=[
            pl.BlockSpec((None, tm, d), lambda b, i: (b, jnp.maximum(i - 1, 0), 0)),
            pl.BlockSpec((None, 6, d), lambda b, i: (b, 0, 0)),
            _const_spec((4, d)),
            _layer_spec(w_qkv, layer),
        ],
        out_specs=pl.BlockSpec((None, tm, n), lambda b, i: (b, i, 0)),
        out_shape=jax.ShapeDtypeStruct((bsz, seq + tm, n), BF16),
        compiler_params=_params("arbitrary", "arbitrary"),
        name="sb_qkv",
    )(h, mod_l, ng_l, w_qkv)


def _split2(x):
    hi = x.astype(BF16)
    return hi, (x - hi.astype(F32)).astype(BF16)


def _sb_attn_kernel(q_ref, k_ref, v_ref, st_ref, o_ref, acc_ref, car_ref, *, scale):
    qb, past = ATT_Q, ATT_PAST
    hd = LANES // 2
    pairs = q_ref.shape[1] // LANES
    heads = 2 * pairs
    qi = pl.program_id(2)
    q0 = pl.multiple_of(qi * qb, qb)
    q = q_ref[...].astype(F32) * scale
    lane = lax.broadcasted_iota(jnp.int32, (qb, LANES), 1)
    q_pairs = []
    for p in range(pairs):
        qp = q[:, p * LANES:(p + 1) * LANES]
        q_pairs.append(jnp.concatenate(
            [jnp.where(lane < hd, qp, 0.0), jnp.where(lane >= hd, qp, 0.0)],
            axis=0).astype(BF16))
    causal = lane < lax.broadcasted_iota(jnp.int32, (qb, LANES), 0)

    def window(k_win, v_win, car, diag):
        blocks = k_win.shape[0] // LANES
        log_beta = [[None] * blocks for _ in range(heads)]
        lhs = [[None] * heads for _ in range(blocks)]
        for p in range(pairs):
            z = lax.dot_general(q_pairs[p], k_win[:, p * LANES:(p + 1) * LANES],
                                (((1,), (1,)), ((), ())), preferred_element_type=F32)
            for h in (2 * p, 2 * p + 1):
                for b in range(blocks):
                    zp = z[(h - 2 * p) * qb:(h - 2 * p + 1) * qb, b * LANES:(b + 1) * LANES]
                    sp = jnp.log(1.0 + jnp.exp2(jnp.abs(zp) * -LOG2E))
                    log_beta[h][b] = jnp.minimum(zp, 0.0) - sp
                    neg_keep = jnp.maximum(zp, 0.0) + sp
                    if diag and b == blocks - 1:
                        neg_keep = jnp.where(causal, neg_keep, 0.0)
                    lhs[b][h] = jnp.concatenate(_split2(neg_keep), axis=1)
        sums = jnp.dot(
            jnp.concatenate([lhs[b][h] for h in range(heads)
                             for b in reversed(range(blocks))], axis=0),
            st_ref[...], preferred_element_type=F32)
        out, new_car = [], []
        for p in range(pairs):
            w_rows = []
            for h in (2 * p, 2 * p + 1):
                after = None if car is None else car[h]
                w_blocks = [None] * blocks
                for b in reversed(range(blocks)):
                    slot = h * blocks + (blocks - 1 - b)
                    part = sums[slot * qb:(slot + 1) * qb]
                    later = part[:, :LANES] if after is None else part[:, :LANES] + after
                    w = jnp.exp(log_beta[h][b] - later)
                    if diag and b == blocks - 1:
                        w = jnp.where(causal, w, 0.0)
                    w_blocks[b] = w.astype(BF16)
                    after = part[:, LANES:] if after is None else after + part[:, LANES:]
                w_rows.append(jnp.concatenate(w_blocks, axis=1))
                new_car.append(after)
            out.append(jnp.dot(jnp.concatenate(w_rows, axis=0),
                               v_win[:, p * LANES:(p + 1) * LANES],
                               preferred_element_type=F32))
        return out, new_car

    def lowest(car):
        low = car[0]
        for c in car[1:]:
            low = jnp.minimum(low, c)
        return jnp.min(low)

    start = pl.multiple_of(q0 + (ATT_PAD - past), qb)
    out, car = window(k_ref[pl.ds(start, past + qb), :],
                      v_ref[pl.ds(start, past + qb), :], None, True)
    for p in range(pairs):
        acc_ref[2 * p * qb:(2 * p + 2) * qb, :] = out[p]
    for h in range(heads):
        car_ref[h * qb:(h + 1) * qb, :] = car[h]

    def cond(carry):
        ws, low = carry
        return jnp.logical_and(ws > ATT_PAD - past, low < ATT_CUTOFF)

    def body(carry):
        ws, _ = carry
        ws = pl.multiple_of(ws, qb)
        car = [car_ref[h * qb:(h + 1) * qb, :] for h in range(heads)]
        out, car = window(k_ref[pl.ds(ws, past), :], v_ref[pl.ds(ws, past), :], car, False)
        for p in range(pairs):
            acc_ref[2 * p * qb:(2 * p + 2) * qb, :] += out[p]
        for h in range(heads):
            car_ref[h * qb:(h + 1) * qb, :] = car[h]
        return ws - past, lowest(car)

    lax.while_loop(cond, body, (q0 + (ATT_PAD - 2 * past), lowest(car)))
    for p in range(pairs):
        o_ref[:, p * LANES:(p + 1) * LANES] = jnp.where(
            lane < hd, acc_ref[2 * p * qb:(2 * p + 1) * qb, :],
            acc_ref[(2 * p + 1) * qb:(2 * p + 2) * qb, :]).astype(BF16)


def _suffix_matrix(width):
    j = np.arange(width)[:, None]
    s = np.arange(width + LANES)[None, :]
    st = ((s >= width) | (j > s)).astype(np.float32)
    return jnp.asarray(np.concatenate([st, st], axis=0), BF16)


def _sb_attention(qkv, d):
    bsz, padded, _ = qkv.shape
    seq = padded - ATT_PAD
    qb, past = ATT_Q, ATT_PAST
    head_dim = d // SB_HEADS
    assert 2 * head_dim == LANES and ATT_PAD == 2 * past and past % qb == 0
    width = ATT_HEAD_PAIRS * LANES
    groups = d // width
    kern = functools.partial(_sb_attn_kernel, scale=head_dim ** -0.5)
    return pl.pallas_call(
        kern,
        grid=(bsz, groups, seq // qb),
        in_specs=[
            pl.BlockSpec((None, qb, width), lambda b, p, i: (b, i + ATT_PAD // qb, p)),
            pl.BlockSpec((None, padded, width), lambda b, p, i: (b, 0, groups + p)),
            pl.BlockSpec((None, padded, width), lambda b, p, i: (b, 0, 2 * groups + p)),
            _const_spec((2 * LANES, 2 * LANES)),
        ],
        out_specs=pl.BlockSpec((None, qb, width), lambda b, p, i: (b, i, p)),
        out_shape=jax.ShapeDtypeStruct((bsz, seq, d), BF16),
        scratch_shapes=[pltpu.VMEM((2 * ATT_HEAD_PAIRS * qb, LANES), F32),
                        pltpu.VMEM((2 * ATT_HEAD_PAIRS * qb, LANES), F32)],
        compiler_params=_params("arbitrary", "arbitrary", "arbitrary"),
        name="sb_attention",
    )(qkv, qkv, qkv, _suffix_matrix(LANES))


def _resid_ffn_kernel(*refs, has_wo, ff_chunk):
    if has_wo:
        h_ref, m_ref, mod_ref, ng_ref, wo_ref, wg_ref, wu_ref, wd_ref, o_ref = refs
        m = jnp.dot(m_ref[...], wo_ref[...], preferred_element_type=F32)
    else:
        h_ref, m_ref, mod_ref, ng_ref, wg_ref, wu_ref, wd_ref, o_ref = refs
        m = m_ref[...]
    h1 = h_ref[...] + mod_ref[2:3, :] * _rms(m, ng_ref[1:2, :])
    u = _prenorm(h1, ng_ref, mod_ref, 2, 3).astype(BF16)
    d_ff = wg_ref.shape[1]
    acc = jnp.zeros_like(h1)
    for c in range(d_ff // ff_chunk):
        lo, hi = c * ff_chunk, (c + 1) * ff_chunk
        g = jnp.dot(u, wg_ref[:, lo:hi], preferred_element_type=F32)
        up = jnp.dot(u, wu_ref[:, lo:hi], preferred_element_type=F32)
        a = (g * _sigmoid(g) * up).astype(BF16)
        acc = acc + jnp.dot(a, wd_ref[lo:hi, :], preferred_element_type=F32)
    o_ref[...] = h1 + mod_ref[5:6, :] * _rms(acc, ng_ref[3:4, :])


def _resid_ffn(h, m, mod_l, ng_l, w_o, o_layer, w_gate, w_up, w_down, layer):
    bsz, seq, d = h.shape
    d_ff = w_gate.shape[2]
    tm = TOK_TILE
    ff_chunk = d_ff // 11
    assert ff_chunk % LANES == 0
    has_wo = w_o is not None
    tile = lambda: pl.BlockSpec((None, tm, d), lambda b, i: (b, i, 0))
    in_specs = [tile(), tile(),
                pl.BlockSpec((None, 6, d), lambda b, i: (b, 0, 0)),
                _const_spec((4, d))]
    args = [h, m, mod_l, ng_l]
    if has_wo:
        in_specs.append(_layer_spec(w_o, o_layer))
        args.append(w_o)
    in_specs += [_layer_spec(w, layer) for w in (w_gate, w_up, w_down)]
    args += [w_gate, w_up, w_down]
    return pl.pallas_call(
        functools.partial(_resid_ffn_kernel, has_wo=has_wo, ff_chunk=ff_chunk),
        grid=(bsz, seq // tm),
        in_specs=in_specs,
        out_specs=tile(),
        out_shape=jax.ShapeDtypeStruct((bsz, seq, d), F32),
        compiler_params=_params("arbitrary", "arbitrary"),
        name="resid_ffn_wo" if has_wo else "resid_ffn",
    )(*args)


def _s5_param_kernel(lr_ref, li_ref, ldt_ref, br_ref, bi_ref,
                     bbr_ref, bbi_ref, pow_ref):
    n = lr_ref.shape[1]
    lr, li = lr_ref[...], li_ref[...]
    dt = jnp.exp(ldt_ref[...])
    mag = jnp.exp(lr * dt)
    ar, ai = mag * jnp.cos(li * dt), mag * jnp.sin(li * dt)
    den = lr * lr + li * li
    er = ((ar - 1.0) * lr + ai * li) / den
    ei = (ai * lr - (ar - 1.0) * li) / den
    br, bi = br_ref[...], bi_ref[...]
    bbr_ref[...] = er * br - ei * bi
    bbi_ref[...] = er * bi + ei * br
    pr, pi = ar, ai
    for k in range(pow_ref.shape[0]):
        pow_ref[k:k + 1, 0:n] = pr
        pow_ref[k:k + 1, n:2 * n] = pi
        pr, pi = pr * ar - pi * ai, pr * ai + pi * ar


def _s5_kernel(h_ref, mod_ref, ng_ref, perm_ref, permt_ref, wbu_ref, pow_ref,
               wcr_ref, wci_ref, dskip_ref, wglu_ref, bglu_ref, o_ref,
               x_ref, xb_ref, yp_ref, seg_ref, carry_ref):
    lc, d = h_ref.shape
    ns = x_ref.shape[1] // 2
    seg = lc // SUBLANES
    lg = S5_LANE_GROUP

    @pl.when(pl.program_id(1) == 0)
    def _():
        carry_ref[...] = jnp.zeros_like(carry_ref)

    u = _prenorm(h_ref[...], ng_ref, mod_ref, 0, 0)
    u_perm = jnp.dot(perm_ref[...], u.astype(BF16),
                     preferred_element_type=F32).astype(BF16)

    n_blocks = wbu_ref.shape[0]
    blocks_per_part = n_blocks // 2
    bu_width = 2 * LANES
    n_out = wcr_ref.shape[0]
    kw = ns // n_out
    apr_all, api_all = pow_ref[seg - 1:seg, 0:ns], pow_ref[seg - 1:seg, ns:2 * ns]
    for g in range(ns // lg):
        re = slice(g * lg, (g + 1) * lg)
        im = slice(ns + g * lg, ns + (g + 1) * lg)

        for part in range(2):
            for cb in range(g * lg // bu_width, (g + 1) * lg // bu_width):
                slab = cb // 2
                col = part * ns + cb * bu_width
                x_ref[:, col:col + bu_width] = jnp.dot(
                    u_perm[:, slab * LANES:(slab + 1) * LANES],
                    wbu_ref[part * blocks_per_part + cb], preferred_element_type=F32)

        ar = jnp.broadcast_to(pow_ref[0:1, re], (SUBLANES, lg))
        ai = jnp.broadcast_to(pow_ref[0:1, im], (SUBLANES, lg))

        xr = xi = jnp.zeros((SUBLANES, lg), F32)
        for k in range(seg):
            rows = slice(k * SUBLANES, (k + 1) * SUBLANES)
            xr, xi = (ar * xr - ai * xi + x_ref[rows, re],
                      ar * xi + ai * xr + x_ref[rows, im])

        apr, api = apr_all[:, g * lg:(g + 1) * lg], api_all[:, g * lg:(g + 1) * lg]
        sr, si = carry_ref[0:1, re], carry_ref[0:1, im]
        for j in range(SUBLANES):
            seg_ref[j:j + 1, re] = sr
            seg_ref[j:j + 1, im] = si
            er, ei = xr[j:j + 1], xi[j:j + 1]
            sr, si = apr * sr - api * si + er, apr * si + api * sr + ei
        carry_ref[0:1, re] = sr
        carry_ref[0:1, im] = si

        xr, xi = seg_ref[:, re], seg_ref[:, im]
        for m in range(seg // 2):
            out_r, out_i = [], []
            for k in (2 * m, 2 * m + 1):
                rows = slice(k * SUBLANES, (k + 1) * SUBLANES)
                xr, xi = (ar * xr - ai * xi + x_ref[rows, re],
                          ar * xi + ai * xr + x_ref[rows, im])
                out_r.append(xr)
                out_i.append(xi)
            rows2 = slice(m * 2 * SUBLANES, (m + 1) * 2 * SUBLANES)
            xb_ref[rows2, re] = jnp.concatenate(out_r, axis=0).astype(BF16)
            xb_ref[rows2, im] = jnp.concatenate(out_i, axis=0).astype(BF16)

        for ob in range(g * lg // kw, (g + 1) * lg // kw):
            yp_ref[:, ob * LANES:(ob + 1) * LANES] = (
                jnp.dot(xb_ref[:, ob * kw:(ob + 1) * kw], wcr_ref[ob],
                        preferred_element_type=F32)
                + jnp.dot(xb_ref[:, ns + ob * kw:ns + (ob + 1) * kw], wci_ref[ob],
                          preferred_element_type=F32))

    hi, lo = _split2(yp_ref[...])
    pt = permt_ref[...]
    y = (jnp.dot(pt, hi, preferred_element_type=F32)
         + jnp.dot(pt, lo, preferred_element_type=F32))
    y = y + dskip_ref[...] * u
    y = 0.5 * y * (1.0 + jnp.tanh(0.7978845608028654 * (y + 0.044715 * (y * y * y))))
    z = jnp.dot(y.astype(BF16), wglu_ref[...],
                preferred_element_type=F32) + bglu_ref[...]
    o_ref[...] = z[:, :d] * _sigmoid(z[:, d:])


def _s5_mixer(h, mod_l, ng_l, lam_re, lam_im, log_dt, b_re, b_im, c_re, c_im,
              d_skip, w_glu, b_glu):
    bsz, seq, d = h.shape
    g, p = lam_re.shape
    gc = b_re.shape[2]
    ns = g * p
    lc, seg = S5_CHUNK, S5_SEG
    assert gc == S5_GROUP and p == S5_STATE and g * gc == d

    flat = lambda a: a.reshape(1, ns)
    chan_major = lambda a: a.transpose(2, 0, 1).reshape(gc, ns)
    bbr, bbi, powers = pl.pallas_call(
        _s5_param_kernel,
        out_shape=(jax.ShapeDtypeStruct((gc, ns), F32),
                   jax.ShapeDtypeStruct((gc, ns), F32),
                   jax.ShapeDtypeStruct((seg, 2 * ns), F32)),
        compiler_params=pltpu.CompilerParams(vmem_limit_bytes=VMEM_LIMIT),
        name="s5_params",
    )(flat(lam_re), flat(lam_im), flat(jnp.repeat(log_dt, p)),
      chan_major(b_re), chan_major(b_im))

    def bu_blocks(bb):
        t = bb.reshape(gc, ns // 256, 4, p)
        t = jnp.einsum('cbgp,gh->bhcgp', t, jnp.eye(4, dtype=F32))
        t = t.reshape(ns // 256, 4 * gc, 256)
        half = jnp.eye(2, dtype=F32)[jnp.arange(ns // 256) % 2]
        return jnp.einsum('bkn,bq->bqkn', t, half).reshape(ns // 256, LANES, 256)

    wbu = jnp.concatenate([bu_blocks(bbr), bu_blocks(bbi)], axis=0).astype(BF16)

    def c_blocks(cc):
        t = cc.reshape(d // LANES, 8, gc, p)
        t = jnp.einsum('ogcp,gh->ogphc', t, jnp.eye(8, dtype=F32))
        return t.reshape(d // LANES, 8 * p, LANES).astype(BF16)

    wcr, wci = c_blocks(c_re), c_blocks(-c_im)

    r = np.arange(lc)
    perm_np = np.zeros((lc, lc), np.float32)
    perm_np[r, (r % SUBLANES) * seg + r // SUBLANES] = 1.0
    perm = jnp.asarray(perm_np, BF16)
    perm_t = jnp.asarray(perm_np.T, BF16)

    kw = ns // (d // LANES)
    return pl.pallas_call(
        _s5_kernel,
        grid=(bsz, seq // lc),
        in_specs=[
            pl.BlockSpec((None, lc, d), lambda b, i: (b, i, 0)),
            pl.BlockSpec((None, 6, d), lambda b, i: (b, 0, 0)),
            _const_spec((4, d)),
            _const_spec((lc, lc)),
            _const_spec((lc, lc)),
            _const_spec((2 * ns // 256, LANES, 256)),
            _const_spec((seg, 2 * ns)),
            _const_spec((d // LANES, kw, LANES)),
            _const_spec((d // LANES, kw, LANES)),
            _const_spec((1, d)),
            _const_spec((d, 2 * d)),
            _const_spec((1, 2 * d)),
        ],
        out_specs=pl.BlockSpec((None, lc, d), lambda b, i: (b, i, 0)),
        out_shape=jax.ShapeDtypeStruct((bsz, seq, d), F32),
        scratch_shapes=[pltpu.VMEM((lc, 2 * ns), F32),
                        pltpu.VMEM((lc, 2 * ns), BF16),
                        pltpu.VMEM((lc, d), F32),
                        pltpu.VMEM((SUBLANES, 2 * ns), F32),
                        pltpu.VMEM((1, 2 * ns), F32)],
        compiler_params=_params("arbitrary", "arbitrary"),
        name="s5_mixer",
    )(h, mod_l, ng_l, perm, perm_t, wbu, powers, wcr, wci,
      d_skip.reshape(1, d), w_glu.astype(BF16), b_glu.reshape(1, 2 * d))


def _conv_kernel(h_ref, mod_ref, ng_ref, w1_ref, b1_ref, wdw_ref, bdw_ref,
                 lng_ref, lnb_ref, w2_ref, b2_ref, o_ref, buf_ref, sh_ref, acc_ref):
    tm, d = h_ref.shape
    halo = CONV_HALO

    @pl.when(pl.program_id(1) == 0)
    def _():
        buf_ref[0:halo, :] = jnp.zeros((halo, d), F32)

    u = _prenorm(h_ref[...], ng_ref, mod_ref, 0, 0)
    z = jnp.dot(u.astype(BF16), w1_ref[...],
                preferred_element_type=F32) + b1_ref[...]
    buf_ref[halo:halo + tm, :] = z[:, :d] * _sigmoid(z[:, d:])

    width = wdw_ref.shape[0] // SUBLANES
    first = halo - (width - 1)
    span = sh_ref.shape[1]
    for s in range(1, SUBLANES):
        sh_ref[s - 1] = buf_ref[s:s + span, :]

    rg = CONV_ROW_GROUP

    def tap_rows(r, carry):
        r0 = pl.multiple_of(r * rg, rg)
        slabs = rg // SUBLANES
        parts = [jnp.zeros((SUBLANES, d), F32) + bdw_ref[...] for _ in range(slabs)]
        for j in range(width):
            a, s = divmod(first + j, SUBLANES)
            tap = wdw_ref[j * SUBLANES:(j + 1) * SUBLANES, :]
            for i in range(slabs):
                rows = pl.ds(r0 + (a + i) * SUBLANES, SUBLANES)
                src = buf_ref[rows, :] if s == 0 else sh_ref[s - 1, rows, :]
                parts[i] = parts[i] + tap * src
        for i in range(slabs):
            acc_ref[pl.ds(r0 + i * SUBLANES, SUBLANES), :] = parts[i]
        return carry

    lax.fori_loop(0, tm // rg, tap_rows, 0)
    buf_ref[0:halo, :] = buf_ref[tm:tm + halo, :]
    acc = acc_ref[...]

    mu = jnp.mean(acc, axis=-1, keepdims=True)
    cen = acc - mu
    var = jnp.mean(cen * cen, axis=-1, keepdims=True)
    y = cen * lax.rsqrt(var + EPS) * lng_ref[...] + lnb_ref[...]
    y = y * _sigmoid(y)
    o_ref[...] = jnp.dot(y.astype(BF16), w2_ref[...],
                         preferred_element_type=F32) + b2_ref[...]


def _conv_mixer(h, mod_l, ng_l, w_pw1, b_pw1, w_dw, b_dw, ln_g, ln_b, w_pw2, b_pw2):
    bsz, seq, d = h.shape
    tm = TOK_TILE
    width = w_dw.shape[0]
    assert width - 1 <= CONV_HALO <= tm
    row = lambda a: a.reshape(1, -1)
    return pl.pallas_call(
        _conv_kernel,
        grid=(bsz, seq // tm),
        in_specs=[
            pl.BlockSpec((None, tm, d), lambda b, i: (b, i, 0)),
            pl.BlockSpec((None, 6, d), lambda b, i: (b, 0, 0)),
            _const_spec((4, d)),
            _const_spec((d, 2 * d)),
            _const_spec((1, 2 * d)),
            _const_spec((width * SUBLANES, d)),
            _const_spec((1, d)),
            _const_spec((1, d)),
            _const_spec((1, d)),
            _const_spec((d, d)),
            _const_spec((1, d)),
        ],
        out_specs=pl.BlockSpec((None, tm, d), lambda b, i: (b, i, 0)),
        out_shape=jax.ShapeDtypeStruct((bsz, seq, d), F32),
        scratch_shapes=[pltpu.VMEM((CONV_HALO + tm, d), F32),
                        pltpu.VMEM((SUBLANES - 1, CONV_HALO + tm - SUBLANES, d), F32),
                        pltpu.VMEM((tm, d), F32)],
        compiler_params=_params("arbitrary", "arbitrary"),
        name="conformer_conv",
    )(h, mod_l, ng_l, w_pw1.astype(BF16), row(b_pw1), jnp.repeat(w_dw, SUBLANES, axis=0), row(b_dw),
      row(ln_g), row(ln_b), w_pw2.astype(BF16), row(b_pw2))


def kernel(x, c, norm_g, w_mod, b_mod, sb_w_qkv, sb_w_o, s5_lam_re, s5_lam_im, s5_log_dt, s5_b_re, s5_b_im, s5_c_re, s5_c_im, s5_d, s5_w_glu, s5_b_glu, cv_w_pw1, cv_b_pw1, cv_w_dw, cv_b_dw, cv_ln_g, cv_ln_b, cv_w_pw2, cv_b_pw2, ffn_w_gate, ffn_w_up, ffn_w_down):
    depth = norm_g.shape[0]
    d = x.shape[-1]
    mod_all = _modulation(c, w_mod, b_mod)
    qkv_bf, o_bf = sb_w_qkv.astype(BF16), sb_w_o.astype(BF16)
    gate_bf, up_bf, down_bf = (w.astype(BF16) for w in (ffn_w_gate, ffn_w_up, ffn_w_down))
    h = x
    for layer in range(depth):
        kind, j = layer % N_MIXERS, layer // N_MIXERS
        mod_l, ng_l = mod_all[layer], norm_g[layer]
        w_o = None
        if kind == 0:
            qkv = _sb_qkv(h, mod_l, ng_l, qkv_bf, j)
            m = _sb_attention(qkv, d)
            w_o = o_bf
        elif kind == 1:
            m = _s5_mixer(h, mod_l, ng_l, s5_lam_re[j], s5_lam_im[j], s5_log_dt[j],
                          s5_b_re[j], s5_b_im[j], s5_c_re[j], s5_c_im[j], s5_d[j],
                          s5_w_glu[j], s5_b_glu[j])
        else:
            m = _conv_mixer(h, mod_l, ng_l, cv_w_pw1[j], cv_b_pw1[j], cv_w_dw[j],
                            cv_b_dw[j], cv_ln_g[j], cv_ln_b[j], cv_w_pw2[j], cv_b_pw2[j])
        h = _resid_ffn(h, m, mod_l, ng_l, w_o, j, gate_bf, up_bf, down_bf, layer)
    return h
```

```python
import functools

import numpy as np
import jax
import jax.numpy as jnp
from jax import lax
from jax.experimental import pallas as pl
from jax.experimental.pallas import tpu as pltpu

F32 = jnp.float32
BF16 = jnp.bfloat16

EPS = 1e-6
LOG2E = 1.4426950408889634
N_MIXERS = 3
SB_HEADS = 16
S5_GROUP = 16
S5_STATE = 64
CONV_WIDTH = 31

LANES = 128
SUBLANES = 8
VMEM_LIMIT = 56 * 1024 * 1024

TOK_TILE = 512
ATT_Q = 128
ATT_PAST = 256
ATT_MORE = 128
ATT_PAD = 2 * ATT_PAST
ATT_HEAD_PAIRS = 4
ATT_CUTOFF = 106.0
S5_SEG = 64
S5_CHUNK = SUBLANES * S5_SEG
S5_LANE_GROUP = 512
CONV_HALO = 32
CONV_ROW_GROUP = 32


def _params(*sem):
    return pltpu.CompilerParams(dimension_semantics=sem, vmem_limit_bytes=VMEM_LIMIT)


def _const_spec(shape):
    nd = len(shape)
    return pl.BlockSpec(shape, lambda *_: (0,) * nd, pipeline_mode=pl.Buffered(1))


def _layer_spec(stacked, layer):
    _, rows, cols = stacked.shape
    return pl.BlockSpec((None, rows, cols), lambda *_: (layer, 0, 0),
                        pipeline_mode=pl.Buffered(1))


def _rms(x, g):
    ms = jnp.mean(x * x, axis=-1, keepdims=True)
    return x * lax.rsqrt(ms + EPS) * g


def _sigmoid(x):
    return 1.0 / (1.0 + jnp.exp(-x))


def _prenorm(x, ng_ref, mod_ref, norm_row, shift_row):
    y = _rms(x, ng_ref[norm_row:norm_row + 1, :])
    return (y * (1.0 + mod_ref[shift_row + 1:shift_row + 2, :])
            + mod_ref[shift_row:shift_row + 1, :])


def _mod_kernel(c_ref, w_ref, b_ref, o_ref):
    c = c_ref[...]
    o_ref[...] = jnp.dot(c * _sigmoid(c), w_ref[...],
                         preferred_element_type=F32) + b_ref[...]


def _modulation(c, w_mod, b_mod):
    depth, d, e = w_mod.shape
    bsz = c.shape[0]
    rows = SUBLANES
    c_pad = jnp.zeros((rows, d), F32).at[:bsz].set(c)
    tn = d
    out = pl.pallas_call(
        _mod_kernel,
        grid=(depth, e // tn),
        in_specs=[
            pl.BlockSpec((rows, d), lambda l, j: (0, 0)),
            pl.BlockSpec((None, d, tn), lambda l, j: (l, 0, j)),
            pl.BlockSpec((None, 1, tn), lambda l, j: (l, 0, j)),
        ],
        out_specs=pl.BlockSpec((None, rows, tn), lambda l, j: (l, 0, j)),
        out_shape=jax.ShapeDtypeStruct((depth, rows, e), F32),
        compiler_params=_params("arbitrary", "arbitrary"),
        name="adaln_mod",
    )(c_pad, w_mod, b_mod.reshape(depth, 1, e))
    return out[:, :bsz].reshape(depth, bsz, 6, d)


def _sb_qkv_kernel(h_ref, mod_ref, ng_ref, w_ref, o_ref):
    @pl.when(pl.program_id(1) == 0)
    def _():
        o_ref[...] = jnp.zeros_like(o_ref)

    @pl.when(pl.program_id(1) > 0)
    def _():
        u = _prenorm(h_ref[...], ng_ref, mod_ref, 0, 0)
        o_ref[...] = jnp.dot(u.astype(BF16), w_ref[...],
                             preferred_element_type=F32).astype(BF16)


def _sb_qkv(h, mod_l, ng_l, w_qkv, layer):
    bsz, seq, d = h.shape
    n = w_qkv.shape[2]
    tm = ATT_PAD
    return pl.pallas_call(
        _sb_qkv_kernel,
        grid=(bsz, seq // tm + 1),
        in_specs=[
            pl.BlockSpec((None, tm, d), lambda b, i: (b, jnp.maximum(i - 1, 0), 0)),
            pl.BlockSpec((None, 6, d), lambda b, i: (b, 0, 0)),
            _const_spec((4, d)),
            _layer_spec(w_qkv, layer),
        ],
        out_specs=pl.BlockSpec((None, tm, n), lambda b, i: (b, i, 0)),
        out_shape=jax.ShapeDtypeStruct((bsz, seq + tm, n), BF16),
        compiler_params=_params("arbitrary", "arbitrary"),
        name="sb_qkv",
    )(h, mod_l, ng_l, w_qkv)


def _split2(x):
    hi = x.astype(BF16)
    return hi, (x - hi.astype(F32)).astype(BF16)


def _sb_attn_kernel(q_ref, k_ref, v_ref, st_ref, o_ref, acc_ref, car_ref, *, scale):
    qb, past = ATT_Q, ATT_PAST
    hd = LANES // 2
    pairs = q_ref.shape[1] // LANES
    heads = 2 * pairs
    qi = pl.program_id(2)
    q0 = pl.multiple_of(qi * qb, qb)
    q = q_ref[...].astype(F32) * scale
    lane = lax.broadcasted_iota(jnp.int32, (qb, LANES), 1)
    q_pairs = []
    for p in range(pairs):
        qp = q[:, p * LANES:(p + 1) * LANES]
        q_pairs.append(jnp.concatenate(
            [jnp.where(lane < hd, qp, 0.0), jnp.where(lane >= hd, qp, 0.0)],
            axis=0).astype(BF16))
    causal = lane < lax.broadcasted_iota(jnp.int32, (qb, LANES), 0)

    def window(k_win, v_win, car, diag):
        blocks = k_win.shape[0] // LANES
        log_beta = [[None] * blocks for _ in range(heads)]
        lhs = [[None] * heads for _ in range(blocks)]
        for p in range(pairs):
            z = lax.dot_general(q_pairs[p], k_win[:, p * LANES:(p + 1) * LANES],
                                (((1,), (1,)), ((), ())), preferred_element_type=F32)
            for h in (2 * p, 2 * p + 1):
                for b in range(blocks):
                    zp = z[(h - 2 * p) * qb:(h - 2 * p + 1) * qb, b * LANES:(b + 1) * LANES]
                    sp = jnp.log(1.0 + jnp.exp2(jnp.abs(zp) * -LOG2E))
                    log_beta[h][b] = jnp.minimum(zp, 0.0) - sp
                    neg_keep = zp - log_beta[h][b]
                    if diag and b == blocks - 1:
                        neg_keep = jnp.where(causal, neg_keep, 0.0)
                    lhs[b][h] = jnp.concatenate(_split2(neg_keep), axis=1)
        sums = jnp.dot(
            jnp.concatenate([lhs[b][h] for h in range(heads)
                             for b in reversed(range(blocks))], axis=0),
            st_ref[...], preferred_element_type=F32)
        out, new_car = [], []
        for p in range(pairs):
            w_rows = []
            for h in (2 * p, 2 * p + 1):
                after = None if car is None else car[h]
                w_blocks = [None] * blocks
                for b in reversed(range(blocks)):
                    slot = h * blocks + (blocks - 1 - b)
                    part = sums[slot * qb:(slot + 1) * qb]
                    later = part[:, :LANES] if after is None else part[:, :LANES] + after
                    w = jnp.exp(log_beta[h][b] - later)
                    if diag and b == blocks - 1:
                        w = jnp.where(causal, w, 0.0)
                    w_blocks[b] = w.astype(BF16)
                    after = part[:, LANES:] if after is None else after + part[:, LANES:]
                w_rows.append(jnp.concatenate(w_blocks, axis=1))
                new_car.append(after)
            out.append(jnp.dot(jnp.concatenate(w_rows, axis=0),
                               v_win[:, p * LANES:(p + 1) * LANES],
                               preferred_element_type=F32))
        return out, new_car

    def lowest(car):
        low = car[0]
        for c in car[1:]:
            low = jnp.minimum(low, c)
        return jnp.min(low)

    start = pl.multiple_of(q0 + (ATT_PAD - past), qb)
    out, car = window(k_ref[pl.ds(start, past + qb), :],
                      v_ref[pl.ds(start, past + qb), :], None, True)
    for p in range(pairs):
        acc_ref[2 * p * qb:(2 * p + 2) * qb, :] = out[p]
    for h in range(heads):
        car_ref[h * qb:(h + 1) * qb, :] = car[h]

    more = ATT_MORE

    def cond(carry):
        ws, low = carry
        return jnp.logical_and(ws > ATT_PAD - more, low < ATT_CUTOFF)

    def body(carry):
        ws, _ = carry
        ws = pl.multiple_of(ws, more)
        car = [car_ref[h * qb:(h + 1) * qb, :] for h in range(heads)]
        out, car = window(k_ref[pl.ds(ws, more), :], v_ref[pl.ds(ws, more), :], car, False)
        for p in range(pairs):
            acc_ref[2 * p * qb:(2 * p + 2) * qb, :] += out[p]
        for h in range(heads):
            car_ref[h * qb:(h + 1) * qb, :] = car[h]
        return ws - more, lowest(car)

    lax.while_loop(cond, body, (q0 + (ATT_PAD - past - more), lowest(car)))
    for p in range(pairs):
        o_ref[:, p * LANES:(p + 1) * LANES] = jnp.where(
            lane < hd, acc_ref[2 * p * qb:(2 * p + 1) * qb, :],
            acc_ref[(2 * p + 1) * qb:(2 * p + 2) * qb, :]).astype(BF16)


def _suffix_matrix(width):
    j = np.arange(width)[:, None]
    s = np.arange(width + LANES)[None, :]
    st = ((s >= width) | (j > s)).astype(np.float32)
    return jnp.asarray(np.concatenate([st, st], axis=0), BF16)


def _sb_attention(qkv, d):
    bsz, padded, _ = qkv.shape
    seq = padded - ATT_PAD
    qb, past = ATT_Q, ATT_PAST
    head_dim = d // SB_HEADS
    assert 2 * head_dim == LANES and past % qb == 0 and ATT_MORE % qb == 0
    assert ATT_PAD >= past + ATT_MORE and ATT_PAD % ATT_MORE == 0
    width = ATT_HEAD_PAIRS * LANES
    groups = d // width
    kern = functools.partial(_sb_attn_kernel, scale=head_dim ** -0.5)
    return pl.pallas_call(
        kern,
        grid=(bsz, groups, seq // qb),
        in_specs=[
            pl.BlockSpec((None, qb, width), lambda b, p, i: (b, i + ATT_PAD // qb, p)),
            pl.BlockSpec((None, padded, width), lambda b, p, i: (b, 0, groups + p)),
            pl.BlockSpec((None, padded, width), lambda b, p, i: (b, 0, 2 * groups + p)),
            _const_spec((2 * LANES, 2 * LANES)),
        ],
        out_specs=pl.BlockSpec((None, qb, width), lambda b, p, i: (b, i, p)),
        out_shape=jax.ShapeDtypeStruct((bsz, seq, d), BF16),
        scratch_shapes=[pltpu.VMEM((2 * ATT_HEAD_PAIRS * qb, LANES), F32),
                        pltpu.VMEM((2 * ATT_HEAD_PAIRS * qb, LANES), F32)],
        compiler_params=_params("arbitrary", "arbitrary", "arbitrary"),
        name="sb_attention",
    )(qkv, qkv, qkv, _suffix_matrix(LANES))


def _resid_ffn_kernel(*refs, has_wo, ff_chunk):
    if has_wo:
        h_ref, m_ref, mod_ref, ng_ref, wo_ref, wg_ref, wu_ref, wd_ref, o_ref = refs
        m = jnp.dot(m_ref[...], wo_ref[...], preferred_element_type=F32)
    else:
        h_ref, m_ref, mod_ref, ng_ref, wg_ref, wu_ref, wd_ref, o_ref = refs
        m = m_ref[...]
    h1 = h_ref[...] + mod_ref[2:3, :] * _rms(m, ng_ref[1:2, :])
    u = _prenorm(h1, ng_ref, mod_ref, 2, 3).astype(BF16)
    d_ff = wg_ref.shape[1]
    acc = jnp.zeros_like(h1)
    for c in range(d_ff // ff_chunk):
        lo, hi = c * ff_chunk, (c + 1) * ff_chunk
        g = jnp.dot(u, wg_ref[:, lo:hi], preferred_element_type=F32)
        up = jnp.dot(u, wu_ref[:, lo:hi], preferred_element_type=F32)
        a = (g * _sigmoid(g) * up).astype(BF16)
        acc = acc + jnp.dot(a, wd_ref[lo:hi, :], preferred_element_type=F32)
    o_ref[...] = h1 + mod_ref[5:6, :] * _rms(acc, ng_ref[3:4, :])


def _resid_ffn(h, m, mod_l, ng_l, w_o, o_layer, w_gate, w_up, w_down, layer):
    bsz, seq, d = h.shape
    d_ff = w_gate.shape[2]
    tm = TOK_TILE
    ff_chunk = d_ff // 11
    assert ff_chunk % LANES == 0
    has_wo = w_o is not None
    tile = lambda: pl.BlockSpec((None, tm, d), lambda b, i: (b, i, 0))
    in_specs = [tile(), tile(),
                pl.BlockSpec((None, 6, d), lambda b, i: (b, 0, 0)),
                _const_spec((4, d))]
    args = [h, m, mod_l, ng_l]
    if has_wo:
        in_specs.append(_layer_spec(w_o, o_layer))
        args.append(w_o)
    in_specs += [_layer_spec(w, layer) for w in (w_gate, w_up, w_down)]
    args += [w_gate, w_up, w_down]
    return pl.pallas_call(
        functools.partial(_resid_ffn_kernel, has_wo=has_wo, ff_chunk=ff_chunk),
        grid=(bsz, seq // tm),
        in_specs=in_specs,
        out_specs=tile(),
        out_shape=jax.ShapeDtypeStruct((bsz, seq, d), F32),
        compiler_params=_params("arbitrary", "arbitrary"),
        name="resid_ffn_wo" if has_wo else "resid_ffn",
    )(*args)


def _s5_param_kernel(lr_ref, li_ref, ldt_ref, br_ref, bi_ref,
                     bbr_ref, bbi_ref, pow_ref):
    n = lr_ref.shape[1]
    lr, li = lr_ref[...], li_ref[...]
    dt = jnp.exp(ldt_ref[...])
    mag = jnp.exp(lr * dt)
    ar, ai = mag * jnp.cos(li * dt), mag * jnp.sin(li * dt)
    den = lr * lr + li * li
    er = ((ar - 1.0) * lr + ai * li) / den
    ei = (ai * lr - (ar - 1.0) * li) / den
    br, bi = br_ref[...], bi_ref[...]
    bbr_ref[...] = er * br - ei * bi
    bbi_ref[...] = er * bi + ei * br
    pr, pi = ar, ai
    for k in range(pow_ref.shape[0]):
        pow_ref[k:k + 1, 0:n] = pr
        pow_ref[k:k + 1, n:2 * n] = pi
        pr, pi = pr * ar - pi * ai, pr * ai + pi * ar


def _s5_kernel(h_ref, mod_ref, ng_ref, perm_ref, permt_ref, wbu_ref, pow_ref,
               wcr_ref, wci_ref, dskip_ref, wglu_ref, bglu_ref, o_ref,
               x_ref, xb_ref, yp_ref, seg_ref, carry_ref):
    lc, d = h_ref.shape
    ns = x_ref.shape[1] // 2
    seg = lc // SUBLANES
    lg = S5_LANE_GROUP

    @pl.when(pl.program_id(1) == 0)
    def _():
        carry_ref[...] = jnp.zeros_like(carry_ref)

    u = _prenorm(h_ref[...], ng_ref, mod_ref, 0, 0)
    u_perm = jnp.dot(perm_ref[...], u.astype(BF16),
                     preferred_element_type=F32).astype(BF16)

    n_blocks = wbu_ref.shape[0]
    blocks_per_part = n_blocks // 2
    bu_width = 2 * LANES
    n_out = wcr_ref.shape[0]
    kw = ns // n_out
    apr_all, api_all = pow_ref[seg - 1:seg, 0:ns], pow_ref[seg - 1:seg, ns:2 * ns]
    for g in range(ns // lg):
        re = slice(g * lg, (g + 1) * lg)
        im = slice(ns + g * lg, ns + (g + 1) * lg)

        for part in range(2):
            for cb in range(g * lg // bu_width, (g + 1) * lg // bu_width):
                slab = cb // 2
                col = part * ns + cb * bu_width
                x_ref[:, col:col + bu_width] = jnp.dot(
                    u_perm[:, slab * LANES:(slab + 1) * LANES],
                    wbu_ref[part * blocks_per_part + cb], preferred_element_type=F32)

        ar = jnp.broadcast_to(pow_ref[0:1, re], (SUBLANES, lg))
        ai = jnp.broadcast_to(pow_ref[0:1, im], (SUBLANES, lg))

        xr = xi = jnp.zeros((SUBLANES, lg), F32)
        for k in range(seg):
            rows = slice(k * SUBLANES, (k + 1) * SUBLANES)
            xr, xi = (ar * xr - ai * xi + x_ref[rows, re],
                      ar * xi + ai * xr + x_ref[rows, im])

        apr, api = apr_all[:, g * lg:(g + 1) * lg], api_all[:, g * lg:(g + 1) * lg]
        sr, si = carry_ref[0:1, re], carry_ref[0:1, im]
        for j in range(SUBLANES):
            seg_ref[j:j + 1, re] = sr
            seg_ref[j:j + 1, im] = si
            er, ei = xr[j:j + 1], xi[j:j + 1]
            sr, si = apr * sr - api * si + er, apr * si + api * sr + ei
        carry_ref[0:1, re] = sr
        carry_ref[0:1, im] = si

        xr, xi = seg_ref[:, re], seg_ref[:, im]
        for m in range(seg // 2):
            out_r, out_i = [], []
            for k in (2 * m, 2 * m + 1):
                rows = slice(k * SUBLANES, (k + 1) * SUBLANES)
                xr, xi = (ar * xr - ai * xi + x_ref[rows, re],
                          ar * xi + ai * xr + x_ref[rows, im])
                out_r.append(xr)
                out_i.append(xi)
            rows2 = slice(m * 2 * SUBLANES, (m + 1) * 2 * SUBLANES)
            xb_ref[rows2, re] = jnp.concatenate(out_r, axis=0).astype(BF16)
            xb_ref[rows2, im] = jnp.concatenate(out_i, axis=0).astype(BF16)

        for ob in range(g * lg // kw, (g + 1) * lg // kw):
            yp_ref[:, ob * LANES:(ob + 1) * LANES] = (
                jnp.dot(xb_ref[:, ob * kw:(ob + 1) * kw], wcr_ref[ob],
                        preferred_element_type=F32)
                + jnp.dot(xb_ref[:, ns + ob * kw:ns + (ob + 1) * kw], wci_ref[ob],
                          preferred_element_type=F32))

    hi, lo = _split2(yp_ref[...])
    pt = permt_ref[...]
    y = (jnp.dot(pt, hi, preferred_element_type=F32)
         + jnp.dot(pt, lo, preferred_element_type=F32))
    y = y + dskip_ref[...] * u
    y = 0.5 * y * (1.0 + jnp.tanh(0.7978845608028654 * (y + 0.044715 * (y * y * y))))
    z = jnp.dot(y.astype(BF16), wglu_ref[...],
                preferred_element_type=F32) + bglu_ref[...]
    o_ref[...] = z[:, :d] * _sigmoid(z[:, d:])


def _s5_mixer(h, mod_l, ng_l, lam_re, lam_im, log_dt, b_re, b_im, c_re, c_im,
              d_skip, w_glu, b_glu):
    bsz, seq, d = h.shape
    g, p = lam_re.shape
    gc = b_re.shape[2]
    ns = g * p
    lc, seg = S5_CHUNK, S5_SEG
    assert gc == S5_GROUP and p == S5_STATE and g * gc == d

    flat = lambda a: a.reshape(1, ns)
    chan_major = lambda a: a.transpose(2, 0, 1).reshape(gc, ns)
    bbr, bbi, powers = pl.pallas_call(
        _s5_param_kernel,
        out_shape=(jax.ShapeDtypeStruct((gc, ns), F32),
                   jax.ShapeDtypeStruct((gc, ns), F32),
                   jax.ShapeDtypeStruct((seg, 2 * ns), F32)),
        compiler_params=pltpu.CompilerParams(vmem_limit_bytes=VMEM_LIMIT),
        name="s5_params",
    )(flat(lam_re), flat(lam_im), flat(jnp.repeat(log_dt, p)),
      chan_major(b_re), chan_major(b_im))

    def bu_blocks(bb):
        t = bb.reshape(gc, ns // 256, 4, p)
        t = jnp.einsum('cbgp,gh->bhcgp', t, jnp.eye(4, dtype=F32))
        t = t.reshape(ns // 256, 4 * gc, 256)
        half = jnp.eye(2, dtype=F32)[jnp.arange(ns // 256) % 2]
        return jnp.einsum('bkn,bq->bqkn', t, half).reshape(ns // 256, LANES, 256)

    wbu = jnp.concatenate([bu_blocks(bbr), bu_blocks(bbi)], axis=0).astype(BF16)

    def c_blocks(cc):
        t = cc.reshape(d // LANES, 8, gc, p)
        t = jnp.einsum('ogcp,gh->ogphc', t, jnp.eye(8, dtype=F32))
        return t.reshape(d // LANES, 8 * p, LANES).astype(BF16)

    wcr, wci = c_blocks(c_re), c_blocks(-c_im)

    r = np.arange(lc)
    perm_np = np.zeros((lc, lc), np.float32)
    perm_np[r, (r % SUBLANES) * seg + r // SUBLANES] = 1.0
    perm = jnp.asarray(perm_np, BF16)
    perm_t = jnp.asarray(perm_np.T, BF16)

    kw = ns // (d // LANES)
    return pl.pallas_call(
        _s5_kernel,
        grid=(bsz, seq // lc),
        in_specs=[
            pl.BlockSpec((None, lc, d), lambda b, i: (b, i, 0)),
            pl.BlockSpec((None, 6, d), lambda b, i: (b, 0, 0)),
            _const_spec((4, d)),
            _const_spec((lc, lc)),
            _const_spec((lc, lc)),
            _const_spec((2 * ns // 256, LANES, 256)),
            _const_spec((seg, 2 * ns)),
            _const_spec((d // LANES, kw, LANES)),
            _const_spec((d // LANES, kw, LANES)),
            _const_spec((1, d)),
            _const_spec((d, 2 * d)),
            _const_spec((1, 2 * d)),
        ],
        out_specs=pl.BlockSpec((None, lc, d), lambda b, i: (b, i, 0)),
        out_shape=jax.ShapeDtypeStruct((bsz, seq, d), F32),
        scratch_shapes=[pltpu.VMEM((lc, 2 * ns), F32),
                        pltpu.VMEM((lc, 2 * ns), BF16),
                        pltpu.VMEM((lc, d), F32),
                        pltpu.VMEM((SUBLANES, 2 * ns), F32),
                        pltpu.VMEM((1, 2 * ns), F32)],
        compiler_params=_params("arbitrary", "arbitrary"),
        name="s5_mixer",
    )(h, mod_l, ng_l, perm, perm_t, wbu, powers, wcr, wci,
      d_skip.reshape(1, d), w_glu.astype(BF16), b_glu.reshape(1, 2 * d))


def _conv_kernel(h_ref, mod_ref, ng_ref, w1_ref, b1_ref, wdw_ref, bdw_ref,
                 lng_ref, lnb_ref, w2_ref, b2_ref, o_ref, buf_ref, sh_ref, acc_ref):
    tm, d = h_ref.shape
    halo = CONV_HALO

    @pl.when(pl.program_id(1) == 0)
    def _():
        buf_ref[0:halo, :] = jnp.zeros((halo, d), F32)

    u = _prenorm(h_ref[...], ng_ref, mod_ref, 0, 0)
    z = jnp.dot(u.astype(BF16), w1_ref[...],
                preferred_element_type=F32) + b1_ref[...]
    buf_ref[halo:halo + tm, :] = z[:, :d] * _sigmoid(z[:, d:])

    width = wdw_ref.shape[0] // SUBLANES
    first = halo - (width - 1)
    span = sh_ref.shape[1]
    for s in range(1, SUBLANES):
        sh_ref[s - 1] = buf_ref[s:s + span, :]

    rg = CONV_ROW_GROUP

    def tap_rows(r, carry):
        r0 = pl.multiple_of(r * rg, rg)
        slabs = rg // SUBLANES
        parts = [jnp.zeros((SUBLANES, d), F32) + bdw_ref[...] for _ in range(slabs)]
        for j in range(width):
            a, s = divmod(first + j, SUBLANES)
            tap = wdw_ref[j * SUBLANES:(j + 1) * SUBLANES, :]
            for i in range(slabs):
                rows = pl.ds(r0 + (a + i) * SUBLANES, SUBLANES)
                src = buf_ref[rows, :] if s == 0 else sh_ref[s - 1, rows, :]
                parts[i] = parts[i] + tap * src
        for i in range(slabs):
            acc_ref[pl.ds(r0 + i * SUBLANES, SUBLANES), :] = parts[i]
        return carry

    lax.fori_loop(0, tm // rg, tap_rows, 0)
    buf_ref[0:halo, :] = buf_ref[tm:tm + halo, :]
    acc = acc_ref[...]

    mu = jnp.mean(acc, axis=-1, keepdims=True)
    cen = acc - mu
    var = jnp.mean(cen * cen, axis=-1, keepdims=True)
    y = cen * lax.rsqrt(var + EPS) * lng_ref[...] + lnb_ref[...]
    y = y * _sigmoid(y)
    o_ref[...] = jnp.dot(y.astype(BF16), w2_ref[...],
                         preferred_element_type=F32) + b2_ref[...]


def _conv_mixer(h, mod_l, ng_l, w_pw1, b_pw1, w_dw, b_dw, ln_g, ln_b, w_pw2, b_pw2):
    bsz, seq, d = h.shape
    tm = TOK_TILE
    width = w_dw.shape[0]
    assert width - 1 <= CONV_HALO <= tm
    row = lambda a: a.reshape(1, -1)
    return pl.pallas_call(
        _conv_kernel,
        grid=(bsz, seq // tm),
        in_specs=[
            pl.BlockSpec((None, tm, d), lambda b, i: (b, i, 0)),
            pl.BlockSpec((None, 6, d), lambda b, i: (b, 0, 0)),
            _const_spec((4, d)),
            _const_spec((d, 2 * d)),
            _const_spec((1, 2 * d)),
            _const_spec((width * SUBLANES, d)),
            _const_spec((1, d)),
            _const_spec((1, d)),
            _const_spec((1, d)),
            _const_spec((d, d)),
            _const_spec((1, d)),
        ],
        out_specs=pl.BlockSpec((None, tm, d), lambda b, i: (b, i, 0)),
        out_shape=jax.ShapeDtypeStruct((bsz, seq, d), F32),
        scratch_shapes=[pltpu.VMEM((CONV_HALO + tm, d), F32),
                        pltpu.VMEM((SUBLANES - 1, CONV_HALO + tm - SUBLANES, d), F32),
                        pltpu.VMEM((tm, d), F32)],
        compiler_params=_params("arbitrary", "arbitrary"),
        name="conformer_conv",
    )(h, mod_l, ng_l, w_pw1.astype(BF16), row(b_pw1), jnp.repeat(w_dw, SUBLANES, axis=0), row(b_dw),
      row(ln_g), row(ln_b), w_pw2.astype(BF16), row(b_pw2))


def kernel(x, c, norm_g, w_mod, b_mod, sb_w_qkv, sb_w_o, s5_lam_re, s5_lam_im, s5_log_dt, s5_b_re, s5_b_im, s5_c_re, s5_c_im, s5_d, s5_w_glu, s5_b_glu, cv_w_pw1, cv_b_pw1, cv_w_dw, cv_b_dw, cv_ln_g, cv_ln_b, cv_w_pw2, cv_b_pw2, ffn_w_gate, ffn_w_up, ffn_w_down):
    depth = norm_g.shape[0]
    d = x.shape[-1]
    mod_all = _modulation(c, w_mod, b_mod)
    qkv_bf, o_bf = sb_w_qkv.astype(BF16), sb_w_o.astype(BF16)
    gate_bf, up_bf, down_bf = (w.astype(BF16) for w in (ffn_w_gate, ffn_w_up, ffn_w_down))
    h = x
    for layer in range(depth):
        kind, j = layer % N_MIXERS, layer // N_MIXERS
        mod_l, ng_l = mod_all[layer], norm_g[layer]
        w_o = None
        if kind == 0:
            qkv = _sb_qkv(h, mod_l, ng_l, qkv_bf, j)
            m = _sb_attention(qkv, d)
            w_o = o_bf
        elif kind == 1:
            m = _s5_mixer(h, mod_l, ng_l, s5_lam_re[j], s5_lam_im[j], s5_log_dt[j],
                          s5_b_re[j], s5_b_im[j], s5_c_re[j], s5_c_im[j], s5_d[j],
                          s5_w_glu[j], s5_b_glu[j])
        else:
            m = _conv_mixer(h, mod_l, ng_l, cv_w_pw1[j], cv_b_pw1[j], cv_w_dw[j],
                            cv_b_dw[j], cv_ln_g[j], cv_ln_b[j], cv_w_pw2[j], cv_b_pw2[j])
        h = _resid_ffn(h, m, mod_l, ng_l, w_o, j, gate_bf, up_bf, down_bf, layer)
    return h
```
